```python
import math
import jax, jax.numpy as jnp
from jax import lax
import numpy as np

D_MODEL = 1024
BATCH = 8
SEQ = 2048
DEPTH = 4
DEC_BATCH = 128
DEC_SEQ = 1
PAST_LEN = 16384
PAGE_SIZE = 128

D_CONV = 512
H_M = 4
DH_M = 256
D_M = H_M * DH_M
CONV_W = 3
D_FF = 2816
CHUNK = 128
EPS = 1e-5
ALPHA = (2.0 * DEPTH) ** 0.25
BETA = (8.0 * DEPTH) ** -0.25

SPLITS = (D_CONV, D_CONV, D_CONV, D_M, D_M, D_M, D_M, H_M, H_M, D_MODEL, D_MODEL)
D_IN = sum(SPLITS)
SPLIT_IDX = tuple(int(s) for s in np.cumsum(SPLITS)[:-1])

kernel_name = "hybrid_shortconv_mlstm_convffn_deepnorm_step"


def layer_norm(x, g, b):
    xf = x.astype(jnp.float32)
    mu = jnp.mean(xf, axis=-1, keepdims=True)
    var = jnp.mean(jnp.square(xf - mu), axis=-1, keepdims=True)
    y = (xf - mu) * lax.rsqrt(var + EPS) * g.astype(jnp.float32) + b.astype(jnp.float32)
    return y.astype(x.dtype)


def causal_dwconv(u, buf, w):
    t = u.shape[1]
    full = jnp.concatenate([buf.astype(u.dtype), u], axis=1)
    y = full[:, 0:t] * w[0]
    for j in range(1, CONV_W):
        y = y + full[:, j:j + t] * w[j]
    return y, full[:, -(CONV_W - 1):]


def mlstm_chunked(q, k, v, li, lf, c0, n0, m0):
    bsz, t, h, _ = q.shape
    dv = v.shape[-1]
    L = CHUNK if t % CHUNK == 0 else t
    nc = t // L
    f32 = jnp.float32

    def seq_chunks(a):
        return a.astype(f32).reshape(bsz, nc, L, h, -1).transpose(1, 0, 3, 2, 4)

    def gate_chunks(a):
        return a.reshape(bsz, nc, L, h).transpose(1, 0, 3, 2)

    xs = (seq_chunks(q), seq_chunks(k), seq_chunks(v), gate_chunks(li), gate_chunks(lf))
    causal = jnp.tril(jnp.ones((L, L), dtype=bool))

    def step(carry, inp):
        c, n, m = carry
        qc, kc, vc, lic, lfc = inp
        b = jnp.cumsum(lfc, axis=-1)
        d = b[..., :, None] - b[..., None, :] + lic[..., None, :]
        d = jnp.where(causal, d, -jnp.inf)
        m_inter = b + m[..., None]
        m_t = jnp.maximum(m_inter, jnp.max(d, axis=-1))
        s = jnp.einsum('bhtd,bhsd->bhts', qc, kc) * jnp.exp(d - m_t[..., None])
        scale_inter = jnp.exp(m_inter - m_t)
        numer = (jnp.einsum('bhts,bhsv->bhtv', s, vc)
                 + scale_inter[..., None] * jnp.einsum('bhtk,bhkv->bhtv', qc, c))
        den = jnp.sum(s, axis=-1) + scale_inter * jnp.einsum('bhtk,bhk->bht', qc, n)
        hc = numer / jnp.maximum(jnp.abs(den), jnp.exp(-m_t))[..., None]
        b_last = b[..., -1]
        w = b_last[..., None] - b + lic
        m_new = jnp.maximum(b_last + m, jnp.max(w, axis=-1))
        decay = jnp.exp(b_last + m - m_new)
        w = jnp.exp(w - m_new[..., None])
        c_new = decay[..., None, None] * c + jnp.einsum('bhsk,bhsv->bhkv', kc * w[..., None], vc)
        n_new = decay[..., None] * n + jnp.einsum('bhs,bhsk->bhk', w, kc)
        return (c_new, n_new, m_new), hc

    (c, n, m), hs = lax.scan(step, (c0.astype(f32), n0.astype(f32), m0.astype(f32)), xs)
    h_out = hs.transpose(1, 0, 3, 2, 4).reshape(bsz, t, h, dv)
    return h_out, c, n, m


def trunk_layer(x, sconv_buf, c0, n0, m0, ffn_buf,
                w_in, b_igate, b_fgate, w_conv_mix, mhln_g, w_proj_a, w_proj_b, w_mix_out,
                ln1_g, ln1_b, w_ffn_up, w_ffn_conv, w_ffn_down, ln2_g, ln2_b):
    bsz, t, _ = x.shape
    z = x @ w_in
    bg, cg, xv, q, k, v, og, ig, fg, ga, gb = jnp.split(z, SPLIT_IDX, axis=-1)

    conv_out, sconv_new = causal_dwconv(cg * xv, sconv_buf, w_conv_mix)
    ya = (bg * conv_out) @ w_proj_a

    q = q.reshape(bsz, t, H_M, DH_M)
    k = k.reshape(bsz, t, H_M, DH_M) * (DH_M ** -0.5)
    v = v.reshape(bsz, t, H_M, DH_M)
    li = (ig + b_igate).astype(jnp.float32)
    lf = jax.nn.log_sigmoid((fg + b_fgate).astype(jnp.float32))
    hm, c_new, n_new, m_new = mlstm_chunked(q, k, v, li, lf, c0, n0, m0)
    mu = jnp.mean(hm, axis=-1, keepdims=True)
    var = jnp.mean(jnp.square(hm - mu), axis=-1, keepdims=True)
    hm = ((hm - mu) * lax.rsqrt(var + EPS)).reshape(bsz, t, D_M) * mhln_g.astype(jnp.float32)
    hm = jax.nn.sigmoid(og) * hm.astype(x.dtype)
    yb = hm @ w_proj_b

    merged = jax.nn.sigmoid(ga) * ya + jax.nn.sigmoid(gb) * yb
    x = layer_norm(ALPHA * x + merged @ w_mix_out, ln1_g, ln1_b)

    up = x @ w_ffn_up
    up_c, ffn_new = causal_dwconv(up, ffn_buf, w_ffn_conv)
    g, u = jnp.split(up_c, 2, axis=-1)
    x = layer_norm(ALPHA * x + (jax.nn.silu(g) * u) @ w_ffn_down, ln2_g, ln2_b)

    return (x, sconv_new, c_new.astype(c0.dtype), n_new.astype(n0.dtype),
            m_new.astype(m0.dtype), ffn_new)


def setup_inputs(seed: int = 0) -> dict:
    key = jax.random.key(seed)
    ks = jax.random.split(key, 32)
    nrm = jax.random.normal
    f32 = jnp.float32
    x_prompt = nrm(ks[0], (BATCH, SEQ, D_MODEL), f32)
    x_sample = nrm(ks[1], (DEC_BATCH, DEC_SEQ, D_MODEL), f32)
    cache_sconv = 0.5 * nrm(ks[2], (DEPTH, DEC_BATCH, CONV_W - 1, D_CONV), f32)
    state_mlstm_C = 0.1 * nrm(ks[3], (DEPTH, DEC_BATCH, H_M, DH_M, DH_M), f32)
    state_mlstm_n = 0.1 * nrm(ks[4], (DEPTH, DEC_BATCH, H_M, DH_M), f32)
    state_mlstm_m = jax.random.uniform(ks[5], (DEPTH, DEC_BATCH, H_M), f32, 0.0, 5.0)
    cache_ffn_conv = nrm(ks[6], (DEPTH, DEC_BATCH, CONV_W - 1, 2 * D_FF), f32)

    w_in = nrm(ks[7], (DEPTH, D_MODEL, D_IN), f32) * D_MODEL ** -0.5
    b_igate = 0.1 * nrm(ks[8], (DEPTH, H_M), f32)
    b_fgate = jnp.linspace(3.0, 6.0, H_M, dtype=f32)[None, :] + 0.1 * nrm(ks[9], (DEPTH, H_M), f32)
    w_conv_mix = 0.5 * nrm(ks[10], (DEPTH, CONV_W, D_CONV), f32)
    mhln_g = 1.0 + 0.1 * nrm(ks[11], (DEPTH, D_M), f32)
    w_proj_a = nrm(ks[12], (DEPTH, D_CONV, D_MODEL), f32) * D_CONV ** -0.5
    w_proj_b = nrm(ks[13], (DEPTH, D_M, D_MODEL), f32) * D_M ** -0.5
    w_mix_out = nrm(ks[14], (DEPTH, D_MODEL, D_MODEL), f32) * (D_MODEL ** -0.5) * BETA
    ln1_g = 1.0 + 0.05 * nrm(ks[15], (DEPTH, D_MODEL), f32)
    ln1_b = 0.02 * nrm(ks[16], (DEPTH, D_MODEL), f32)
    w_ffn_up = nrm(ks[17], (DEPTH, D_MODEL, 2 * D_FF), f32) * D_MODEL ** -0.5
    w_ffn_conv = 0.5 * nrm(ks[18], (DEPTH, CONV_W, 2 * D_FF), f32)
    w_ffn_down = nrm(ks[19], (DEPTH, D_FF, D_MODEL), f32) * (D_FF ** -0.5) * BETA
    ln2_g = 1.0 + 0.05 * nrm(ks[20], (DEPTH, D_MODEL), f32)
    ln2_b = 0.02 * nrm(ks[21], (DEPTH, D_MODEL), f32)
    return {
        "x_prompt": x_prompt, "x_sample": x_sample,
        "cache_sconv": cache_sconv, "state_mlstm_C": state_mlstm_C,
        "state_mlstm_n": state_mlstm_n, "state_mlstm_m": state_mlstm_m,
        "cache_ffn_conv": cache_ffn_conv,
        "w_in": w_in, "b_igate": b_igate, "b_fgate": b_fgate, "w_conv_mix": w_conv_mix,
        "mhln_g": mhln_g, "w_proj_a": w_proj_a, "w_proj_b": w_proj_b, "w_mix_out": w_mix_out,
        "ln1_g": ln1_g, "ln1_b": ln1_b, "w_ffn_up": w_ffn_up, "w_ffn_conv": w_ffn_conv,
        "w_ffn_down": w_ffn_down, "ln2_g": ln2_g, "ln2_b": ln2_b,
    }


def reference(x_prompt, x_sample, cache_sconv, state_mlstm_C, state_mlstm_n, state_mlstm_m,
              cache_ffn_conv, w_in, b_igate, b_fgate, w_conv_mix, mhln_g, w_proj_a, w_proj_b,
              w_mix_out, ln1_g, ln1_b, w_ffn_up, w_ffn_conv, w_ffn_down, ln2_g, ln2_b):
    dt = x_prompt.dtype
    bsz = x_prompt.shape[0]
    zero_sconv = jnp.zeros((bsz, CONV_W - 1, D_CONV), dt)
    zero_c = jnp.zeros((bsz, H_M, DH_M, DH_M), dt)
    zero_n = jnp.zeros((bsz, H_M, DH_M), dt)
    zero_m = jnp.zeros((bsz, H_M), dt)
    zero_ffn = jnp.zeros((bsz, CONV_W - 1, 2 * D_FF), dt)

    xp, xs = x_prompt, x_sample
    sp_l, ss_l, cp_l, cs_l, np_l, ns_l, mp_l, ms_l, fp_l, fs_l = ([] for _ in range(10))
    for l in range(DEPTH):
        params = (w_in[l], b_igate[l], b_fgate[l], w_conv_mix[l], mhln_g[l], w_proj_a[l],
                  w_proj_b[l], w_mix_out[l], ln1_g[l], ln1_b[l], w_ffn_up[l], w_ffn_conv[l],
                  w_ffn_down[l], ln2_g[l], ln2_b[l])
        xp, sp, cp, np_, mp, fp = trunk_layer(xp, zero_sconv, zero_c, zero_n, zero_m, zero_ffn, *params)
        xs, ss, cs, ns, ms, fs = trunk_layer(xs, cache_sconv[l], state_mlstm_C[l], state_mlstm_n[l],
                                             state_mlstm_m[l], cache_ffn_conv[l], *params)
        sp_l.append(sp); ss_l.append(ss); cp_l.append(cp); cs_l.append(cs)
        np_l.append(np_); ns_l.append(ns); mp_l.append(mp); ms_l.append(ms)
        fp_l.append(fp); fs_l.append(fs)

    return (xp, xs,
            jnp.stack(sp_l), jnp.stack(ss_l),
            jnp.stack(cp_l), jnp.stack(cs_l),
            jnp.stack(np_l), jnp.stack(ns_l),
            jnp.stack(mp_l), jnp.stack(ms_l),
            jnp.stack(fp_l), jnp.stack(fs_l))
```

```python
import functools

import jax
import jax.numpy as jnp
from jax import lax
from jax.experimental import pallas as pl
from jax.experimental.pallas import tpu as pltpu

F32 = jnp.float32
BF16 = jnp.bfloat16

D_MODEL = 1024
D_CONV = 512
H_M = 4
DH_M = 256
D_M = H_M * DH_M
CONV_W = 3
D_FF = 2816
EPS = 1e-5
DEPTH = 4
ALPHA = (2.0 * DEPTH) ** 0.25
K_SCALE = DH_M ** -0.5

SUBLANES = 8
LANES = 128
VMEM_LIMIT_BYTES = 56 * 1024 * 1024

CHUNK = 128
TILE_T = 512
FF_CHUNK = 256
N_FF_CHUNKS = D_FF // FF_CHUNK
SEQ_BLOCK = 8
HALO_ROW = SUBLANES - (CONV_W - 1)


def _dot(a, b):
    return jnp.dot(a, b, preferred_element_type=F32)


def _dot_nt(a, b):
    return lax.dot_general(a, b, (((1,), (1,)), ((), ())), preferred_element_type=F32)


def _dot_tn(a, b):
    return lax.dot_general(a, b, (((0,), (0,)), ((), ())), preferred_element_type=F32)


def _sigmoid(x):
    return 1.0 / (1.0 + jnp.exp(-x))


def _log_sigmoid(x):
    return -(jnp.maximum(-x, 0.0) + jnp.log1p(jnp.exp(-jnp.abs(x))))


def _layer_norm(y, g, b):
    mu = jnp.mean(y, axis=-1, keepdims=True)
    yc = y - mu
    var = jnp.mean(yc * yc, axis=-1, keepdims=True)
    return yc * lax.rsqrt(var + EPS) * g + b


def _head_norm(h):
    mu = jnp.mean(h, axis=-1, keepdims=True)
    hc = h - mu
    var = jnp.mean(hc * hc, axis=-1, keepdims=True)
    return hc * lax.rsqrt(var + EPS)


def _resident(shape, layer):
    nd = len(shape)
    return pl.BlockSpec((None,) + tuple(shape[1:]),
                        lambda *_: (layer,) + (0,) * (nd - 1),
                        pipeline_mode=pl.Buffered(1))


def _mixer_prompt_kernel(x_ref, wa_ref, wqkvo_ref, wgt_ref, gbias_ref, wg_ref, wconv_ref, mhg_ref,
                         wpa_ref, wpb_ref, wmix_ref, lng_ref, lnb_ref,
                         x1_ref, sconv_ref, c_ref, n_ref, m_ref,
                         xb_ref, pbuf_ref, merged_ref, zq_ref, hm_ref, srow_ref, scol_ref):
    t = pl.program_id(1)
    tile = x_ref.shape[0]
    n_chunks = tile // CHUNK

    @pl.when(t == 0)
    def _init():
        pbuf_ref[0:SUBLANES, :] = jnp.zeros((SUBLANES, D_CONV), F32)
        c_ref[...] = jnp.zeros(c_ref.shape, F32)
        n_ref[...] = jnp.zeros(n_ref.shape, F32)
        m_ref[...] = jnp.zeros(m_ref.shape, F32)

    x = x_ref[...]
    xb_ref[...] = x.astype(BF16)
    xb = xb_ref[...]

    za = _dot(xb, wa_ref[...])
    bg = za[:, 0:D_CONV]
    p = za[:, D_CONV:2 * D_CONV] * za[:, 2 * D_CONV:3 * D_CONV]
    pbuf_ref[SUBLANES:SUBLANES + tile, :] = p
    wc = wconv_ref[...]
    conv = (pbuf_ref[HALO_ROW:HALO_ROW + tile, :] * wc[0:1, :]
            + pbuf_ref[HALO_ROW + 1:HALO_ROW + 1 + tile, :] * wc[1:2, :]
            + p * wc[2:3, :])
    halo = pbuf_ref[tile:tile + SUBLANES, :]
    pbuf_ref[0:SUBLANES, :] = halo
    sconv_ref[...] = halo
    ya = _dot((bg * conv).astype(BF16), wpa_ref[...])
    ga = _dot(xb, wg_ref[:, 0:D_MODEL])
    merged_ref[...] = _sigmoid(ga) * ya

    gates = _dot_nt(wgt_ref[...], xb) + gbias_ref[...]
    row_id = lax.broadcasted_iota(jnp.int32, (2 * H_M, CHUNK), 0)
    lane_id = lax.broadcasted_iota(jnp.int32, (2 * H_M, CHUNK), 1)
    for c in range(n_chunks):
        g = gates[:, c * CHUNK:(c + 1) * CHUNK]
        lf = _log_sigmoid(g)
        csum = lf
        shift = 1
        while shift < CHUNK:
            csum = csum + jnp.where(lane_id >= shift, pltpu.roll(csum, shift, 1), 0.0)
            shift *= 2
        srow_ref[c] = jnp.where(row_id < H_M, g, csum)
    for c in range(n_chunks):
        padded = jnp.concatenate(
            [srow_ref[c], jnp.zeros((LANES - 2 * H_M, CHUNK), F32)], axis=0)
        scol_ref[c * CHUNK:(c + 1) * CHUNK, :] = padded.T

    tri = (lax.broadcasted_iota(jnp.int32, (CHUNK, CHUNK), 0)
           >= lax.broadcasted_iota(jnp.int32, (CHUNK, CHUNK), 1))

    for h in range(H_M):
        zq_ref[...] = _dot(xb_ref[...], wqkvo_ref[:, h * 4 * DH_M:(h + 1) * 4 * DH_M])
        mhg = mhg_ref[:, h * DH_M:(h + 1) * DH_M]

        def chunk_body(c, carry, h=h, mhg=mhg):
            r0 = pl.multiple_of(c * CHUNK, CHUNK)
            rows = pl.ds(r0, CHUNK)
            q = zq_ref[rows, 0:DH_M]
            k = zq_ref[rows, DH_M:2 * DH_M] * K_SCALE
            v = zq_ref[rows, 2 * DH_M:3 * DH_M]
            o = zq_ref[rows, 3 * DH_M:4 * DH_M]
            qb = q.astype(BF16)
            kb = k.astype(BF16)
            vb = v.astype(BF16)
            srow = srow_ref[c]
            li_row = srow[h:h + 1, :]
            b_row = srow[H_M + h:H_M + h + 1, :]
            scol = scol_ref[rows, :]
            li_col = scol[:, h:h + 1]
            b_col = scol[:, H_M + h:H_M + h + 1]
            m0 = m_ref[h:h + 1, 0:1]
            c0 = c_ref[h]
            n0 = n_ref[h:h + 1, :]

            d = jnp.where(tri, b_col - b_row + li_row, -jnp.inf)
            m_inter = b_col + m0
            m_t = jnp.maximum(m_inter, jnp.max(d, axis=-1, keepdims=True))
            s = _dot_nt(qb, kb) * jnp.exp(d - m_t)
            scale_inter = jnp.exp(m_inter - m_t)
            numer = _dot(s.astype(BF16), vb) + scale_inter * _dot(qb, c0.astype(BF16))
            den = (jnp.sum(s, axis=-1, keepdims=True)
                   + scale_inter * jnp.sum(q * n0, axis=-1, keepdims=True))
            hc = numer / jnp.maximum(jnp.abs(den), jnp.exp(-m_t))

            b_last = b_row[:, CHUNK - 1:CHUNK]
            w_row = b_last - b_row + li_row
            m_new = jnp.maximum(b_last + m0, jnp.max(w_row, axis=-1, keepdims=True))
            decay = jnp.exp(b_last + m0 - m_new)
            w_col = jnp.exp(b_last - b_col + li_col - m_new)
            kw = k * w_col
            c_ref[h] = decay * c0 + _dot_tn(kw.astype(BF16), vb)
            n_ref[h:h + 1, :] = decay * n0 + jnp.sum(kw, axis=0, keepdims=True)
            m_ref[h:h + 1, :] = jnp.broadcast_to(m_new, (1, LANES))

            hm = _sigmoid(o) * (_head_norm(hc) * mhg)
            hm_ref[rows, h * DH_M:(h + 1) * DH_M] = hm.astype(BF16)
            return carry

        lax.fori_loop(0, n_chunks, chunk_body, 0)

    yb = _dot(hm_ref[...], wpb_ref[...])
    gb = _dot(xb_ref[...], wg_ref[:, D_MODEL:2 * D_MODEL])
    merged = merged_ref[...] + _sigmoid(gb) * yb
    y = ALPHA * x_ref[...] + _dot(merged.astype(BF16), wmix_ref[...])
    x1_ref[...] = _layer_norm(y, lng_ref[...], lnb_ref[...])


def _mixer_prompt(x, layer, w):
    bsz, seq, _ = x.shape
    tile = min(TILE_T, seq)
    n_tiles = seq // tile
    n_chunks = tile // CHUNK
    row_block = lambda b, t: (b, t, 0)
    per_seq = lambda b, t: (b, 0, 0)
    in_specs = [
        pl.BlockSpec((None, tile, D_MODEL), row_block),
        _resident(w["wa"].shape, layer),
        _resident(w["wqkvo"].shape, layer),
        _resident(w["wgt"].shape, layer),
        _resident(w["gbias"].shape, layer),
        _resident(w["wg"].shape, layer),
        _resident(w["wconv"].shape, layer),
        _resident(w["mhg"].shape, layer),
        _resident(w["wpa"].shape, layer),
        _resident(w["wpb"].shape, layer),
        _resident(w["wmix"].shape, layer),
        _resident(w["ln1g"].shape, layer),
        _resident(w["ln1b"].shape, layer),
    ]
    out_shape = (
        jax.ShapeDtypeStruct((bsz, seq, D_MODEL), F32),
        jax.ShapeDtypeStruct((bsz, SUBLANES, D_CONV), F32),
        jax.ShapeDtypeStruct((bsz, H_M, DH_M, DH_M), F32),
        jax.ShapeDtypeStruct((bsz, H_M, DH_M), F32),
        jax.ShapeDtypeStruct((bsz, SUBLANES, LANES), F32),
    )
    out_specs = (
        pl.BlockSpec((None, tile, D_MODEL), row_block),
        pl.BlockSpec((None, SUBLANES, D_CONV), per_seq),
        pl.BlockSpec((None, H_M, DH_M, DH_M), lambda b, t: (b, 0, 0, 0)),
        pl.BlockSpec((None, H_M, DH_M), per_seq),
        pl.BlockSpec((None, SUBLANES, LANES), per_seq),
    )
    scratch = [
        pltpu.VMEM((tile, D_MODEL), BF16),
        pltpu.VMEM((tile + SUBLANES, D_CONV), F32),
        pltpu.VMEM((tile, D_MODEL), F32),
        pltpu.VMEM((tile, 4 * DH_M), F32),
        pltpu.VMEM((tile, D_M), BF16),
        pltpu.VMEM((n_chunks, 2 * H_M, CHUNK), F32),
        pltpu.VMEM((tile, LANES), F32),
    ]
    x1, sconv, c, n, m = pl.pallas_call(
        _mixer_prompt_kernel,
        grid=(bsz, n_tiles),
        in_specs=in_specs,
        out_specs=out_specs,
        out_shape=out_shape,
        scratch_shapes=scratch,
        compiler_params=pltpu.CompilerParams(
            dimension_semantics=("arbitrary", "arbitrary"),
            vmem_limit_bytes=VMEM_LIMIT_BYTES),
        name="mixer_prompt",
    )(x, w["wa"], w["wqkvo"], w["wgt"], w["gbias"], w["wg"], w["wconv"], w["mhg"],
      w["wpa"], w["wpb"], w["wmix"], w["ln1g"], w["ln1b"])
    return x1, sconv[:, HALO_ROW:, :], c, n, m[:, :H_M, 0]


def _ffn_prompt_kernel(x_ref, wgate_ref, wup_ref, wcg_ref, wcu_ref, wdown_ref, lng_ref, lnb_ref,
                       x2_ref, hg_ref, hu_ref,
                       xb_ref, gbuf_ref, ubuf_ref, acc_ref):
    t = pl.program_id(1)
    tile = x_ref.shape[0]

    @pl.when(t == 0)
    def _init():
        hg_ref[...] = jnp.zeros(hg_ref.shape, F32)
        hu_ref[...] = jnp.zeros(hu_ref.shape, F32)

    xb_ref[...] = x_ref[...].astype(BF16)
    acc_ref[...] = jnp.zeros(acc_ref.shape, F32)

    def conv(buf_ref, halo_ref, wc_ref, j, val):
        buf_ref[0:SUBLANES, :] = halo_ref[j]
        buf_ref[SUBLANES:SUBLANES + tile, :] = val
        wc = wc_ref[j]
        out = (buf_ref[HALO_ROW:HALO_ROW + tile, :] * wc[0:1, :]
               + buf_ref[HALO_ROW + 1:HALO_ROW + 1 + tile, :] * wc[1:2, :]
               + val * wc[2:3, :])
        halo_ref[j] = buf_ref[tile:tile + SUBLANES, :]
        return out

    def body(j, carry):
        xb = xb_ref[...]
        g = conv(gbuf_ref, hg_ref, wcg_ref, j, _dot(xb, wgate_ref[j]))
        u = conv(ubuf_ref, hu_ref, wcu_ref, j, _dot(xb, wup_ref[j]))
        hdn = (g * _sigmoid(g) * u).astype(BF16)
        acc_ref[...] += _dot(hdn, wdown_ref[j])
        return carry

    lax.fori_loop(0, N_FF_CHUNKS, body, 0)
    y = ALPHA * x_ref[...] + acc_ref[...]
    x2_ref[...] = _layer_norm(y, lng_ref[...], lnb_ref[...])


def _ffn_prompt(x, layer, w):
    bsz, seq, _ = x.shape
    tile = min(TILE_T, seq)
    n_tiles = seq // tile
    row_block = lambda b, t: (b, t, 0)
    halo_block = lambda b, t: (b, 0, 0, 0)
    in_specs = [
        pl.BlockSpec((None, tile, D_MODEL), row_block),
        _resident(w["wfg"].shape, layer),
        _resident(w["wfu"].shape, layer),
        _resident(w["wcg"].shape, layer),
        _resident(w["wcu"].shape, layer),
        _resident(w["wfd"].shape, layer),
        _resident(w["ln2g"].shape, layer),
        _resident(w["ln2b"].shape, layer),
    ]
    halo_shape = jax.ShapeDtypeStruct((bsz, N_FF_CHUNKS, SUBLANES, FF_CHUNK), F32)
    halo_spec = pl.BlockSpec((None, N_FF_CHUNKS, SUBLANES, FF_CHUNK), halo_block)
    x2, hg, hu = pl.pallas_call(
        _ffn_prompt_kernel,
        grid=(bsz, n_tiles),
        in_specs=in_specs,
        out_specs=(pl.BlockSpec((None, tile, D_MODEL), row_block), halo_spec, halo_spec),
        out_shape=(jax.ShapeDtypeStruct((bsz, seq, D_MODEL), F32), halo_shape, halo_shape),
        scratch_shapes=[
            pltpu.VMEM((tile, D_MODEL), BF16),
            pltpu.VMEM((tile + SUBLANES, FF_CHUNK), F32),
            pltpu.VMEM((tile + SUBLANES, FF_CHUNK), F32),
            pltpu.VMEM((tile, D_MODEL), F32),
        ],
        compiler_params=pltpu.CompilerParams(
            dimension_semantics=("arbitrary", "arbitrary"),
            vmem_limit_bytes=VMEM_LIMIT_BYTES),
        name="ffn_prompt",
    )(x, w["wfg"], w["wfu"], w["wcg"], w["wcu"], w["wfd"], w["ln2g"], w["ln2b"])

    def rows(hb):
        return hb[:, :, HALO_ROW:, :].transpose(0, 2, 1, 3).reshape(bsz, CONV_W - 1, D_FF)

    return x2, jnp.concatenate([rows(hg), rows(hu)], axis=-1)


def _proj_sample_kernel(x_ref, cache_ref, wa_ref, wqkvo_ref, wgc_ref, gbias_ref, wg_ref, wconv_ref,
                        wpa_ref,
                        q_ref, k_ref, v_ref, so_ref, gates_ref, part_ref, sgb_ref, sconv_ref):
    xb = x_ref[...].astype(BF16)
    za = _dot(xb, wa_ref[...])
    bg = za[:, 0:D_CONV]
    p = za[:, D_CONV:2 * D_CONV] * za[:, 2 * D_CONV:3 * D_CONV]
    cache0 = cache_ref[:, 0:D_CONV]
    cache1 = cache_ref[:, D_CONV:2 * D_CONV]
    wc = wconv_ref[...]
    conv = cache0 * wc[0:1, :] + cache1 * wc[1:2, :] + p * wc[2:3, :]
    sconv_ref[:, 0:D_CONV] = cache1
    sconv_ref[:, D_CONV:2 * D_CONV] = p
    ya = _dot((bg * conv).astype(BF16), wpa_ref[...])
    zg = _dot(xb, wg_ref[...])
    part_ref[...] = _sigmoid(zg[:, 0:D_MODEL]) * ya
    sgb_ref[...] = _sigmoid(zg[:, D_MODEL:2 * D_MODEL])
    gz = _dot(xb, wgc_ref[...]) + gbias_ref[...]
    lane = lax.broadcasted_iota(jnp.int32, gz.shape, 1)
    gates_ref[...] = jnp.where(lane < H_M, gz, _log_sigmoid(gz))
    for h in range(H_M):
        zh = _dot(xb, wqkvo_ref[:, h * 4 * DH_M:(h + 1) * 4 * DH_M])
        cols = slice(h * DH_M, (h + 1) * DH_M)
        q_ref[:, cols] = zh[:, 0:DH_M]
        k_ref[:, cols] = zh[:, DH_M:2 * DH_M] * K_SCALE
        v_ref[:, cols] = zh[:, 2 * DH_M:3 * DH_M]
        so_ref[:, cols] = _sigmoid(zh[:, 3 * DH_M:4 * DH_M])


def _proj_sample(x, cache, layer, w):
    ns = x.shape[0]
    whole = lambda shape: pl.BlockSpec(shape, lambda i: (0,) * len(shape))
    act = jax.ShapeDtypeStruct((ns, D_MODEL), F32)
    in_specs = [
        whole((ns, D_MODEL)),
        whole((ns, 2 * D_CONV)),
        _resident(w["wa"].shape, layer),
        _resident(w["wqkvo"].shape, layer),
        _resident(w["wgc"].shape, layer),
        _resident(w["gbias_row"].shape, layer),
        _resident(w["wg"].shape, layer),
        _resident(w["wconv"].shape, layer),
        _resident(w["wpa"].shape, layer),
    ]
    out_shape = (act, act, act, act, jax.ShapeDtypeStruct((ns, LANES), F32), act, act,
                 jax.ShapeDtypeStruct((ns, 2 * D_CONV), F32))
    out_specs = tuple(whole(s.shape) for s in out_shape)
    return pl.pallas_call(
        _proj_sample_kernel,
        grid=(1,),
        in_specs=in_specs,
        out_specs=out_specs,
        out_shape=out_shape,
        compiler_params=pltpu.CompilerParams(
            dimension_semantics=("arbitrary",), vmem_limit_bytes=VMEM_LIMIT_BYTES),
        name="proj_sample",
    )(x, cache, w["wa"], w["wqkvo"], w["wgc"], w["gbias_row"], w["wg"], w["wconv"], w["wpa"])


def _state_sample_kernel(q_ref, k_ref, v_ref, so_ref, gates_ref, m0_ref, n0_ref, c0_ref, mhg_ref,
                         hm_ref, c_ref, n_ref, m_ref, qc_ref):
    nb = q_ref.shape[0]
    gates = gates_ref[...]
    zpad = jnp.zeros((LANES - nb, DH_M), F32)
    for h in range(H_M):
        cols = slice(h * DH_M, (h + 1) * DH_M)
        q = q_ref[:, cols]
        k = k_ref[:, cols]
        v = v_ref[:, cols]
        li = gates[:, h:h + 1]
        lf = gates[:, H_M + h:H_M + h + 1]
        m0 = m0_ref[:, h:h + 1]
        n0 = n0_ref[:, h, :]
        m_new = jnp.maximum(lf + m0, li)
        decay = jnp.exp(lf + m0 - m_new)
        wgt = jnp.exp(li - m_new)
        kw = k * wgt
        q_t = jnp.concatenate([q, zpad], axis=0).T
        kw_t = jnp.concatenate([kw, zpad], axis=0).T
        for i in range(nb):
            c0 = c0_ref[i, h]
            qc_ref[i:i + 1, :] = jnp.sum(q_t[:, i:i + 1] * c0, axis=0, keepdims=True)
            c_ref[i, h] = decay[i:i + 1, :] * c0 + kw_t[:, i:i + 1] * v[i:i + 1, :]
        s = jnp.sum(q * k, axis=-1, keepdims=True) * wgt
        numer = s * v + decay * qc_ref[...]
        den = s + decay * jnp.sum(q * n0, axis=-1, keepdims=True)
        hc = numer / jnp.maximum(jnp.abs(den), jnp.exp(-m_new))
        hm_ref[:, cols] = so_ref[:, cols] * (_head_norm(hc) * mhg_ref[:, cols])
        n_ref[:, h, :] = decay * n0 + kw
        m_ref[:, h:h + 1] = m_new


def _state_sample(q, k, v, so, gates, m0, n0, c0, layer, w):
    ns = q.shape[0]
    nb = min(SEQ_BLOCK, ns)
    rows = lambda i: (i, 0)
    act_spec = pl.BlockSpec((nb, D_MODEL), rows)
    in_specs = [
        act_spec, act_spec, act_spec, act_spec,
        pl.BlockSpec((nb, LANES), rows),
        pl.BlockSpec((None, nb, H_M), lambda i: (layer, i, 0)),
        pl.BlockSpec((None, nb, H_M, DH_M), lambda i: (layer, i, 0, 0)),
        pl.BlockSpec((None, nb, H_M, DH_M, DH_M), lambda i: (layer, i, 0, 0, 0)),
        pl.BlockSpec((None, 1, D_M), lambda i: (layer, 0, 0)),
    ]
    out_shape = (
        jax.ShapeDtypeStruct((ns, D_M), F32),
        jax.ShapeDtypeStruct((ns, H_M, DH_M, DH_M), F32),
        jax.ShapeDtypeStruct((ns, H_M, DH_M), F32),
        jax.ShapeDtypeStruct((ns, H_M), F32),
    )
    out_specs = (
        act_spec,
        pl.BlockSpec((nb, H_M, DH_M, DH_M), lambda i: (i, 0, 0, 0)),
        pl.BlockSpec((nb, H_M, DH_M), lambda i: (i, 0, 0)),
        pl.BlockSpec((nb, H_M), rows),
    )
    return pl.pallas_call(
        _state_sample_kernel,
        grid=(ns // nb,),
        in_specs=in_specs,
        out_specs=out_specs,
        out_shape=out_shape,
        scratch_shapes=[pltpu.VMEM((nb, DH_M), F32)],
        compiler_params=pltpu.CompilerParams(
            dimension_semantics=("arbitrary",), vmem_limit_bytes=VMEM_LIMIT_BYTES),
        name="state_sample",
    )(q, k, v, so, gates, m0, n0, c0, w["mhg"])


def _out_sample_kernel(x_ref, hm_ref, part_ref, sgb_ref, fcache_ref, wpb_ref, wmix_ref, ln1g_ref,
                       ln1b_ref, wgate_ref, wup_ref, wcg_ref, wcu_ref, wdown_ref, ln2g_ref, ln2b_ref,
                       x2_ref, fnew_ref):
    yb = _dot(hm_ref[...].astype(BF16), wpb_ref[...])
    merged = part_ref[...] + sgb_ref[...] * yb
    y = ALPHA * x_ref[...] + _dot(merged.astype(BF16), wmix_ref[...])
    x1 = _layer_norm(y, ln1g_ref[...], ln1b_ref[...])
    xb = x1.astype(BF16)
    acc = jnp.zeros(x1.shape, F32)
    up_w = 2 * D_FF
    for j in range(N_FF_CHUNKS):
        convd = []
        for (w_ref, wc_ref, base) in ((wgate_ref, wcg_ref, 0), (wup_ref, wcu_ref, D_FF)):
            cols = slice(base + j * FF_CHUNK, base + (j + 1) * FF_CHUNK)
            val = _dot(xb, w_ref[j])
            row0 = fcache_ref[:, cols]
            row1 = fcache_ref[:, up_w + cols.start:up_w + cols.stop]
            wc = wc_ref[j]
            convd.append(row0 * wc[0:1, :] + row1 * wc[1:2, :] + val * wc[2:3, :])
            fnew_ref[:, cols] = row1
            fnew_ref[:, up_w + cols.start:up_w + cols.stop] = val
        g, u = convd
        acc = acc + _dot((g * _sigmoid(g) * u).astype(BF16), wdown_ref[j])
    y2 = ALPHA * x1 + acc
    x2_ref[...] = _layer_norm(y2, ln2g_ref[...], ln2b_ref[...])


def _out_sample(x, hm, part, sgb, fcache, layer, w):
    ns = x.shape[0]
    whole = lambda shape: pl.BlockSpec(shape, lambda i: (0,) * len(shape))
    act = whole((ns, D_MODEL))
    fc = whole((ns, 2 * 2 * D_FF))
    names = ("wpb", "wmix", "ln1g", "ln1b", "wfg", "wfu", "wcg", "wcu", "wfd", "ln2g", "ln2b")
    in_specs = [act, act, act, act, fc] + [_resident(w[n].shape, layer) for n in names]
    return pl.pallas_call(
        _out_sample_kernel,
        grid=(1,),
        in_specs=in_specs,
        out_specs=(act, fc),
        out_shape=(jax.ShapeDtypeStruct((ns, D_MODEL), F32),
                   jax.ShapeDtypeStruct((ns, 2 * 2 * D_FF), F32)),
        compiler_params=pltpu.CompilerParams(
            dimension_semantics=("arbitrary",), vmem_limit_bytes=VMEM_LIMIT_BYTES),
        name="out_sample",
    )(x, hm, part, sgb, fcache, *[w[n] for n in names])


def _pack_weights(w_in, b_igate, b_fgate, w_conv_mix, mhln_g, w_proj_a, w_proj_b, w_mix_out,
                  ln1_g, ln1_b, w_ffn_up, w_ffn_conv, w_ffn_down, ln2_g, ln2_b):
    depth = w_in.shape[0]
    o_a = 3 * D_CONV
    o_gate = o_a + 4 * D_M
    o_g = o_gate + 2 * H_M
    heads = [w_in[:, :, o_a + i * D_M:o_a + (i + 1) * D_M].reshape(depth, D_MODEL, H_M, DH_M)
             for i in range(4)]
    wqkvo = jnp.concatenate(heads, axis=-1).reshape(depth, D_MODEL, 4 * D_M)
    wgate = w_in[:, :, o_gate:o_g]
    gbias = jnp.concatenate([b_igate, b_fgate], axis=-1)
    wgc = jnp.pad(wgate, ((0, 0), (0, 0), (0, LANES - 2 * H_M)))
    gbias_row = jnp.pad(gbias, ((0, 0), (0, LANES - 2 * H_M)))[:, None, :]

    def ff_cols(a):
        r = a.shape[1]
        return a.reshape(depth, r, N_FF_CHUNKS, FF_CHUNK).transpose(0, 2, 1, 3)

    row = lambda a: a[:, None, :]
    return {
        "wa": w_in[:, :, 0:o_a].astype(BF16),
        "wqkvo": wqkvo.astype(BF16),
        "wgt": wgate.transpose(0, 2, 1).astype(BF16),
        "gbias": gbias[:, :, None],
        "wgc": wgc.astype(BF16),
        "gbias_row": gbias_row,
        "wg": w_in[:, :, o_g:o_g + 2 * D_MODEL].astype(BF16),
        "wconv": w_conv_mix,
        "mhg": row(mhln_g),
        "wpa": w_proj_a.astype(BF16),
        "wpb": w_proj_b.astype(BF16),
        "wmix": w_mix_out.astype(BF16),
        "ln1g": row(ln1_g), "ln1b": row(ln1_b),
        "wfg": ff_cols(w_ffn_up[:, :, 0:D_FF]).astype(BF16),
        "wfu": ff_cols(w_ffn_up[:, :, D_FF:2 * D_FF]).astype(BF16),
        "wcg": ff_cols(w_ffn_conv[:, :, 0:D_FF]),
        "wcu": ff_cols(w_ffn_conv[:, :, D_FF:2 * D_FF]),
        "wfd": w_ffn_down.reshape(depth, N_FF_CHUNKS, FF_CHUNK, D_MODEL).astype(BF16),
        "ln2g": row(ln2_g), "ln2b": row(ln2_b),
    }


def kernel(x_prompt, x_sample, cache_sconv, state_mlstm_C, state_mlstm_n, state_mlstm_m, cache_ffn_conv, w_in, b_igate, b_fgate, w_conv_mix, mhln_g, w_proj_a, w_proj_b, w_mix_out, ln1_g, ln1_b, w_ffn_up, w_ffn_conv, w_ffn_down, ln2_g, ln2_b):
    depth = w_in.shape[0]
    ns = x_sample.shape[0]
    assert x_sample.shape[1] == 1
    assert x_prompt.shape[1] % CHUNK == 0
    w = _pack_weights(w_in, b_igate, b_fgate, w_conv_mix, mhln_g, w_proj_a, w_proj_b, w_mix_out,
                      ln1_g, ln1_b, w_ffn_up, w_ffn_conv, w_ffn_down, ln2_g, ln2_b)
    sconv_cache = cache_sconv.reshape(depth, ns, (CONV_W - 1) * D_CONV)
    ffn_cache = cache_ffn_conv.reshape(depth, ns, (CONV_W - 1) * 2 * D_FF)

    xp = x_prompt
    xs = x_sample.reshape(ns, D_MODEL)
    outs = [[] for _ in range(10)]
    for l in range(depth):
        xp, sp, cp, np_, mp = _mixer_prompt(xp, l, w)
        xp, fp = _ffn_prompt(xp, l, w)

        q, k, v, so, gates, part, sgb, ss = _proj_sample(xs, sconv_cache[l], l, w)
        hm, cs, ns_, ms = _state_sample(q, k, v, so, gates, state_mlstm_m, state_mlstm_n,
                                        state_mlstm_C, l, w)
        xs, fs = _out_sample(xs, hm, part, sgb, ffn_cache[l], l, w)

        for lst, val in zip(outs, (sp, ss.reshape(ns, CONV_W - 1, D_CONV), cp, cs, np_, ns_, mp, ms,
                                   fp, fs.reshape(ns, CONV_W - 1, 2 * D_FF))):
            lst.append(val)

    return (xp, xs.reshape(ns, 1, D_MODEL)) + tuple(jnp.stack(o) for o in outs)
```

```python
import functools

import jax
import jax.numpy as jnp
from jax import lax
from jax.experimental import pallas as pl
from jax.experimental.pallas import tpu as pltpu

F32 = jnp.float32
BF16 = jnp.bfloat16

D_MODEL = 1024
D_CONV = 512
H_M = 4
DH_M = 256
D_M = H_M * DH_M
CONV_W = 3
D_FF = 2816
EPS = 1e-5
DEPTH = 4
ALPHA = (2.0 * DEPTH) ** 0.25
K_SCALE = DH_M ** -0.5

SUBLANES = 8
LANES = 128
VMEM_LIMIT_BYTES = 56 * 1024 * 1024

CHUNK = 128
TILE_T = 512
FF_CHUNK = 256
N_FF_CHUNKS = D_FF // FF_CHUNK
FFN_RING = 2
SEQ_BLOCK = 8
HALO_ROW = SUBLANES - (CONV_W - 1)


def _dot(a, b):
    return jnp.dot(a, b, preferred_element_type=F32)


def _dot_nt(a, b):
    return lax.dot_general(a, b, (((1,), (1,)), ((), ())), preferred_element_type=F32)


def _dot_tn(a, b):
    return lax.dot_general(a, b, (((0,), (0,)), ((), ())), preferred_element_type=F32)


def _sigmoid(x):
    return 1.0 / (1.0 + jnp.exp(-x))


def _log_sigmoid(x):
    return -(jnp.maximum(-x, 0.0) + jnp.log1p(jnp.exp(-jnp.abs(x))))


def _layer_norm(y, g, b):
    mu = jnp.mean(y, axis=-1, keepdims=True)
    yc = y - mu
    var = jnp.mean(yc * yc, axis=-1, keepdims=True)
    return yc * lax.rsqrt(var + EPS) * g + b


def _head_norm(h):
    mu = jnp.mean(h, axis=-1, keepdims=True)
    hc = h - mu
    var = jnp.mean(hc * hc, axis=-1, keepdims=True)
    return hc * lax.rsqrt(var + EPS)


def _resident(shape, layer):
    nd = len(shape)
    return pl.BlockSpec((None,) + tuple(shape[1:]),
                        lambda *_: (layer,) + (0,) * (nd - 1),
                        pipeline_mode=pl.Buffered(1))


def _mixer_prompt_kernel(x_ref, wa_ref, wqkvo_ref, wgt_ref, gbias_ref, wg_ref, wconv_ref, mhg_ref,
                         wpa_ref, wpb_ref, wmix_ref, lng_ref, lnb_ref,
                         x1_ref, sconv_ref, c_ref, n_ref, m_ref,
                         xb_ref, pbuf_ref, merged_ref, zq_ref, hm_ref, srow_ref, scol_ref):
    t = pl.program_id(1)
    tile = x_ref.shape[0]
    n_chunks = tile // CHUNK

    @pl.when(t == 0)
    def _init():
        pbuf_ref[0:SUBLANES, :] = jnp.zeros((SUBLANES, D_CONV), F32)
        c_ref[...] = jnp.zeros(c_ref.shape, F32)
        n_ref[...] = jnp.zeros(n_ref.shape, F32)
        m_ref[...] = jnp.zeros(m_ref.shape, F32)

    x = x_ref[...]
    xb_ref[...] = x.astype(BF16)
    xb = xb_ref[...]

    za = _dot(xb, wa_ref[...])
    bg = za[:, 0:D_CONV]
    p = za[:, D_CONV:2 * D_CONV] * za[:, 2 * D_CONV:3 * D_CONV]
    pbuf_ref[SUBLANES:SUBLANES + tile, :] = p
    wc = wconv_ref[...]
    conv = (pbuf_ref[HALO_ROW:HALO_ROW + tile, :] * wc[0:1, :]
            + pbuf_ref[HALO_ROW + 1:HALO_ROW + 1 + tile, :] * wc[1:2, :]
            + p * wc[2:3, :])
    halo = pbuf_ref[tile:tile + SUBLANES, :]
    pbuf_ref[0:SUBLANES, :] = halo
    sconv_ref[...] = halo
    ya = _dot((bg * conv).astype(BF16), wpa_ref[...])
    ga = _dot(xb, wg_ref[:, 0:D_MODEL])
    merged_ref[...] = _sigmoid(ga) * ya

    gates = _dot_nt(wgt_ref[...], xb) + gbias_ref[...]
    row_id = lax.broadcasted_iota(jnp.int32, (2 * H_M, CHUNK), 0)
    lane_id = lax.broadcasted_iota(jnp.int32, (2 * H_M, CHUNK), 1)
    for c in range(n_chunks):
        g = gates[:, c * CHUNK:(c + 1) * CHUNK]
        lf = _log_sigmoid(g)
        csum = lf
        shift = 1
        while shift < CHUNK:
            csum = csum + jnp.where(lane_id >= shift, pltpu.roll(csum, shift, 1), 0.0)
            shift *= 2
        srow_ref[c] = jnp.where(row_id < H_M, g, csum)
    for c in range(n_chunks):
        padded = jnp.concatenate(
            [srow_ref[c], jnp.zeros((LANES - 2 * H_M, CHUNK), F32)], axis=0)
        scol_ref[c * CHUNK:(c + 1) * CHUNK, :] = padded.T

    tri = (lax.broadcasted_iota(jnp.int32, (CHUNK, CHUNK), 0)
           >= lax.broadcasted_iota(jnp.int32, (CHUNK, CHUNK), 1))

    for h in range(H_M):
        for i in range(4):
            zq_ref[:, i * DH_M:(i + 1) * DH_M] = _dot(
                xb_ref[...], wqkvo_ref[:, i * D_M + h * DH_M:i * D_M + (h + 1) * DH_M])
        mhg =mhg_ref[:, h * DH_M:(h + 1) * DH_M]

        def chunk_body(c, carry, h=h, mhg=mhg):
            r0 = pl.multiple_of(c * CHUNK, CHUNK)
            rows = pl.ds(r0, CHUNK)
            q = zq_ref[rows, 0:DH_M]
            k = zq_ref[rows, DH_M:2 * DH_M] * K_SCALE
            v = zq_ref[rows, 2 * DH_M:3 * DH_M]
            o = zq_ref[rows, 3 * DH_M:4 * DH_M]
            qb = q.astype(BF16)
            kb = k.astype(BF16)
            vb = v.astype(BF16)
            srow = srow_ref[c]
            li_row = srow[h:h + 1, :]
            b_row = srow[H_M + h:H_M + h + 1, :]
            scol = scol_ref[rows, :]
            li_col = scol[:, h:h + 1]
            b_col = scol[:, H_M + h:H_M + h + 1]
            m0 = m_ref[h:h + 1, 0:1]
            c0 = c_ref[h]
            n0 = n_ref[h:h + 1, :]

            d = jnp.where(tri, b_col - b_row + li_row, -jnp.inf)
            m_inter = b_col + m0
            m_t = jnp.maximum(m_inter, jnp.max(d, axis=-1, keepdims=True))
            s = _dot_nt(qb, kb) * jnp.exp(d - m_t)
            scale_inter = jnp.exp(m_inter - m_t)
            numer = _dot(s.astype(BF16), vb) + scale_inter * _dot(qb, c0.astype(BF16))
            den = (jnp.sum(s, axis=-1, keepdims=True)
                   + scale_inter * jnp.sum(q * n0, axis=-1, keepdims=True))
            hc = numer / jnp.maximum(jnp.abs(den), jnp.exp(-m_t))

            b_last = b_row[:, CHUNK - 1:CHUNK]
            w_row = b_last - b_row + li_row
            m_new = jnp.maximum(b_last + m0, jnp.max(w_row, axis=-1, keepdims=True))
            decay = jnp.exp(b_last + m0 - m_new)
            w_col = jnp.exp(b_last - b_col + li_col - m_new)
            kw = k * w_col
            c_ref[h] = decay * c0 + _dot_tn(kw.astype(BF16), vb)
            n_ref[h:h + 1, :] = decay * n0 + jnp.sum(kw, axis=0, keepdims=True)
            m_ref[h:h + 1, :] = jnp.broadcast_to(m_new, (1, LANES))

            hm = _sigmoid(o) * (_head_norm(hc) * mhg)
            hm_ref[rows, h * DH_M:(h + 1) * DH_M] = hm.astype(BF16)
            return carry

        lax.fori_loop(0, n_chunks, chunk_body, 0)

    yb = _dot(hm_ref[...], wpb_ref[...])
    gb = _dot(xb_ref[...], wg_ref[:, D_MODEL:2 * D_MODEL])
    merged = merged_ref[...] + _sigmoid(gb) * yb
    y = ALPHA * x_ref[...] + _dot(merged.astype(BF16), wmix_ref[...])
    x1_ref[...] = _layer_norm(y, lng_ref[...], lnb_ref[...])


def _mixer_prompt(x, layer, w):
    bsz, seq, _ = x.shape
    tile = min(TILE_T, seq)
    n_tiles = seq // tile
    n_chunks = tile // CHUNK
    row_block = lambda b, t: (b, t, 0)
    per_seq = lambda b, t: (b, 0, 0)
    in_specs = [
        pl.BlockSpec((None, tile, D_MODEL), row_block),
        _resident(w["wa"].shape, layer),
        _resident(w["wqkvo"].shape, layer),
        _resident(w["wgt"].shape, layer),
        _resident(w["gbias"].shape, layer),
        _resident(w["wg"].shape, layer),
        _resident(w["wconv"].shape, layer),
        _resident(w["mhg"].shape, layer),
        _resident(w["wpa"].shape, layer),
        _resident(w["wpb"].shape, layer),
        _resident(w["wmix"].shape, layer),
        _resident(w["ln1g"].shape, layer),
        _resident(w["ln1b"].shape, layer),
    ]
    out_shape = (
        jax.ShapeDtypeStruct((bsz, seq, D_MODEL), F32),
        jax.ShapeDtypeStruct((bsz, SUBLANES, D_CONV), F32),
        jax.ShapeDtypeStruct((bsz, H_M, DH_M, DH_M), F32),
        jax.ShapeDtypeStruct((bsz, H_M, DH_M), F32),
        jax.ShapeDtypeStruct((bsz, SUBLANES, LANES), F32),
    )
    out_specs = (
        pl.BlockSpec((None, tile, D_MODEL), row_block),
        pl.BlockSpec((None, SUBLANES, D_CONV), per_seq),
        pl.BlockSpec((None, H_M, DH_M, DH_M), lambda b, t: (b, 0, 0, 0)),
        pl.BlockSpec((None, H_M, DH_M), per_seq),
        pl.BlockSpec((None, SUBLANES, LANES), per_seq),
    )
    scratch = [
        pltpu.VMEM((tile, D_MODEL), BF16),
        pltpu.VMEM((tile + SUBLANES, D_CONV), F32),
        pltpu.VMEM((tile, D_MODEL), F32),
        pltpu.VMEM((tile, 4 * DH_M), F32),
        pltpu.VMEM((tile, D_M), BF16),
        pltpu.VMEM((n_chunks, 2 * H_M, CHUNK), F32),
        pltpu.VMEM((tile, LANES), F32),
    ]
    x1, sconv, c, n, m = pl.pallas_call(
        _mixer_prompt_kernel,
        grid=(bsz, n_tiles),
        in_specs=in_specs,
        out_specs=out_specs,
        out_shape=out_shape,
        scratch_shapes=scratch,
        compiler_params=pltpu.CompilerParams(
            dimension_semantics=("arbitrary", "arbitrary"),
            vmem_limit_bytes=VMEM_LIMIT_BYTES),
        name="mixer_prompt",
    )(x, w["wa"], w["wqkvo"], w["wgt"], w["gbias"], w["wg"], w["wconv"], w["mhg"],
      w["wpa"], w["wpb"], w["wmix"], w["ln1g"], w["ln1b"])
    return x1, sconv[:, HALO_ROW:, :], c, n, m[:, :H_M, 0]


def _ffn_prompt_kernel(x_ref, wup_ref, wconv_ref, wdown_ref, lng_ref, lnb_ref,
                       x2_ref, halo_ref,
                       xb_ref, acc_ref, hbuf0_ref, hbuf1_ref, *bufs):
    t = pl.program_id(1)
    tile = x_ref.shape[0]

    @pl.when(t == 0)
    def _init():
        halo_ref[...] = jnp.zeros(halo_ref.shape, F32)

    xb_ref[...] = x_ref[...].astype(BF16)

    def up_project(buf_ref, cols):
        val = _dot(xb_ref[...], wup_ref[:, cols])
        buf_ref[0:SUBLANES, :] = halo_ref[:, cols]
        buf_ref[SUBLANES:SUBLANES + tile, :] = val
        halo_ref[:, cols] = val[tile - SUBLANES:tile, :]

    def conv(buf_ref, cols):
        wc = wconv_ref[:, cols]
        return (buf_ref[HALO_ROW:HALO_ROW + tile, :] * wc[0:1, :]
                + buf_ref[HALO_ROW + 1:HALO_ROW + 1 + tile, :] * wc[1:2, :]
                + buf_ref[SUBLANES:SUBLANES + tile, :] * wc[2:3, :])

    n_ring = len(bufs) // 2
    g_cols = lambda j: slice(j * FF_CHUNK, (j + 1) * FF_CHUNK)
    u_cols = lambda j: slice(D_FF + j * FF_CHUNK, D_FF + (j + 1) * FF_CHUNK)
    ring = lambda j: (bufs[2 * (j % n_ring)], bufs[2 * (j % n_ring) + 1])

    def stage(j):
        gbuf, ubuf = ring(j)
        up_project(gbuf, g_cols(j))
        up_project(ubuf, u_cols(j))

    stage(0)
    for j in range(N_FF_CHUNKS):
        gbuf, ubuf = ring(j)
        g = conv(gbuf, g_cols(j))
        u = conv(ubuf, u_cols(j))
        hbuf = (hbuf0_ref, hbuf1_ref)[j % 2]
        hbuf[...] = (g * _sigmoid(g) * u).astype(BF16)
        if j + 1 < N_FF_CHUNKS:
            stage(j + 1)
        part = _dot(hbuf[...], wdown_ref[j * FF_CHUNK:(j + 1) * FF_CHUNK, :])
        if j == 0:
            acc_ref[...] = part
        else:
            acc_ref[...] += part

    y = ALPHA * x_ref[...] + acc_ref[...]
    x2_ref[...] = _layer_norm(y, lng_ref[...], lnb_ref[...])


def _ffn_prompt(x, layer, w):
    bsz, seq, _ = x.shape
    tile = min(TILE_T, seq)
    n_tiles = seq // tile
    row_block = lambda b, t: (b, t, 0)
    in_specs = [
        pl.BlockSpec((None, tile, D_MODEL), row_block),
        _resident(w["wup"].shape, layer),
        _resident(w["wfconv"].shape, layer),
        _resident(w["wdown"].shape, layer),
        _resident(w["ln2g"].shape, layer),
        _resident(w["ln2b"].shape, layer),
    ]
    x2, halo = pl.pallas_call(
        _ffn_prompt_kernel,
        grid=(bsz, n_tiles),
        in_specs=in_specs,
        out_specs=(pl.BlockSpec((None, tile, D_MODEL), row_block),
                   pl.BlockSpec((None, SUBLANES, 2 * D_FF), lambda b, t: (b, 0, 0))),
        out_shape=(jax.ShapeDtypeStruct((bsz, seq, D_MODEL), F32),
                   jax.ShapeDtypeStruct((bsz, SUBLANES, 2 * D_FF), F32)),
        scratch_shapes=[
            pltpu.VMEM((tile, D_MODEL), BF16),
            pltpu.VMEM((tile, D_MODEL), F32),
            pltpu.VMEM((tile, FF_CHUNK), BF16),
            pltpu.VMEM((tile, FF_CHUNK), BF16),
        ] + [pltpu.VMEM((tile + SUBLANES, FF_CHUNK), F32)] * (2 * FFN_RING),
        compiler_params=pltpu.CompilerParams(
            dimension_semantics=("arbitrary", "arbitrary"),
            vmem_limit_bytes=VMEM_LIMIT_BYTES),
        name="ffn_prompt",
    )(x, w["wup"], w["wfconv"], w["wdown"], w["ln2g"], w["ln2b"])
    return x2, halo[:, HALO_ROW:, :]


def _proj_sample_kernel(x_ref, cache_ref, wa_ref, wqkvo_ref, wgc_ref, gbias_ref, wg_ref, wconv_ref,
                        wpa_ref,
                        q_ref, k_ref, v_ref, so_ref, gates_ref, part_ref, sgb_ref, sconv_ref):
    xb = x_ref[...].astype(BF16)
    za = _dot(xb, wa_ref[...])
    bg = za[:, 0:D_CONV]
    p = za[:, D_CONV:2 * D_CONV] * za[:, 2 * D_CONV:3 * D_CONV]
    cache0 = cache_ref[:, 0:D_CONV]
    cache1 = cache_ref[:, D_CONV:2 * D_CONV]
    wc = wconv_ref[...]
    conv = cache0 * wc[0:1, :] + cache1 * wc[1:2, :] + p * wc[2:3, :]
    sconv_ref[:, 0:D_CONV] = cache1
    sconv_ref[:, D_CONV:2 * D_CONV] = p
    ya = _dot((bg * conv).astype(BF16), wpa_ref[...])
    zg = _dot(xb, wg_ref[...])
    part_ref[...] = _sigmoid(zg[:, 0:D_MODEL]) * ya
    sgb_ref[...] = _sigmoid(zg[:, D_MODEL:2 * D_MODEL])
    gz = _dot(xb, wgc_ref[...]) + gbias_ref[...]
    lane = lax.broadcasted_iota(jnp.int32, gz.shape, 1)
    gates_ref[...] = jnp.where(lane < H_M, gz, _log_sigmoid(gz))
    q_ref[...] = _dot(xb, wqkvo_ref[:, 0:D_M])
    k_ref[...] = _dot(xb, wqkvo_ref[:, D_M:2 * D_M]) * K_SCALE
    v_ref[...] = _dot(xb, wqkvo_ref[:, 2 * D_M:3 * D_M])
    so_ref[...] = _sigmoid(_dot(xb, wqkvo_ref[:, 3 * D_M:4 * D_M]))


def _proj_sample(x, cache, layer, w):
    ns = x.shape[0]
    whole = lambda shape: pl.BlockSpec(shape, lambda i: (0,) * len(shape))
    act = jax.ShapeDtypeStruct((ns, D_MODEL), F32)
    in_specs = [
        whole((ns, D_MODEL)),
        whole((ns, 2 * D_CONV)),
        _resident(w["wa"].shape, layer),
        _resident(w["wqkvo"].shape, layer),
        _resident(w["wgc"].shape, layer),
        _resident(w["gbias_row"].shape, layer),
        _resident(w["wg"].shape, layer),
        _resident(w["wconv"].shape, layer),
        _resident(w["wpa"].shape, layer),
    ]
    out_shape = (act, act, act, act, jax.ShapeDtypeStruct((ns, LANES), F32), act, act,
                 jax.ShapeDtypeStruct((ns, 2 * D_CONV), F32))
    out_specs = tuple(whole(s.shape) for s in out_shape)
    return pl.pallas_call(
        _proj_sample_kernel,
        grid=(1,),
        in_specs=in_specs,
        out_specs=out_specs,
        out_shape=out_shape,
        compiler_params=pltpu.CompilerParams(
            dimension_semantics=("arbitrary",), vmem_limit_bytes=VMEM_LIMIT_BYTES),
        name="proj_sample",
    )(x, cache, w["wa"], w["wqkvo"], w["wgc"], w["gbias_row"], w["wg"], w["wconv"], w["wpa"])


def _state_sample_kernel(q_ref, k_ref, v_ref, so_ref, gates_ref, m0_ref, n0_ref, c0_ref, mhg_ref,
                         *rest):
    hm_ref, c_ref, n_ref, m_ref, qc_ref = rest[-5:]
    nb = q_ref.shape[0]
    gates = gates_ref[...]
    zpad = jnp.zeros((LANES - nb, DH_M), F32)
    for h in range(H_M):
        cols = slice(h * DH_M, (h + 1) * DH_M)
        q = q_ref[:, cols]
        k = k_ref[:, cols]
        v = v_ref[:, cols]
        li = gates[:, h:h + 1]
        lf = gates[:, H_M + h:H_M + h + 1]
        m0 = m0_ref[:, h:h + 1]
        n0 = n0_ref[:, h, :]
        m_new = jnp.maximum(lf + m0, li)
        decay = jnp.exp(lf + m0 - m_new)
        wgt = jnp.exp(li - m_new)
        kw = k * wgt
        q_t = jnp.concatenate([q, zpad], axis=0).T
        kw_t = jnp.concatenate([kw, zpad], axis=0).T
        for i in range(nb):
            c0 = c0_ref[i, h]
            qc_ref[i:i + 1, :] = jnp.sum(q_t[:, i:i + 1] * c0, axis=0, keepdims=True)
            c_ref[i, h] = decay[i:i + 1, :] * c0 + kw_t[:, i:i + 1] * v[i:i + 1, :]
        s = jnp.sum(q * k, axis=-1, keepdims=True) * wgt
        numer = s * v + decay * qc_ref[...]
        den = s + decay * jnp.sum(q * n0, axis=-1, keepdims=True)
        hc = numer / jnp.maximum(jnp.abs(den), jnp.exp(-m_new))
        hm_ref[:, cols] = so_ref[:, cols] * (_head_norm(hc) * mhg_ref[:, cols])
        n_ref[:, h, :] = decay * n0 + kw
        m_ref[:, h:h + 1] = m_new


def _state_sample(q, k, v, so, gates, m0, n0, c0, layer, w, c_all):
    depth = c0.shape[0]
    ns = q.shape[0]
    nb = min(SEQ_BLOCK, ns)
    rows = lambda i: (i, 0)
    act_spec = pl.BlockSpec((nb, D_MODEL), rows)
    layer_slab = lambda i: (layer, i, 0, 0, 0)
    in_specs = [
        act_spec, act_spec, act_spec, act_spec,
        pl.BlockSpec((nb, LANES), rows),
        pl.BlockSpec((None, nb, H_M), lambda i: (layer, i, 0)),
        pl.BlockSpec((None, nb, H_M, DH_M), lambda i: (layer, i, 0, 0)),
        pl.BlockSpec((None, nb, H_M, DH_M, DH_M), layer_slab),
        pl.BlockSpec((None, 1, D_M), lambda i: (layer, 0, 0)),
    ]
    args = [q, k, v, so, gates, m0, n0, c0, w["mhg"]]
    aliases = {}
    if c_all is not None:
        in_specs.append(pl.BlockSpec(memory_space=pl.ANY))
        args.append(c_all)
        aliases = {len(args) - 1: 1}
    out_shape = (
        jax.ShapeDtypeStruct((ns, D_M), F32),
        jax.ShapeDtypeStruct((depth, ns, H_M, DH_M, DH_M), F32),
        jax.ShapeDtypeStruct((ns, H_M, DH_M), F32),
        jax.ShapeDtypeStruct((ns, H_M), F32),
    )
    out_specs = (
        act_spec,
        pl.BlockSpec((None, nb, H_M, DH_M, DH_M), layer_slab),
        pl.BlockSpec((nb, H_M, DH_M), lambda i: (i, 0, 0)),
        pl.BlockSpec((nb, H_M), rows),
    )
    return pl.pallas_call(
        _state_sample_kernel,
        grid=(ns // nb,),
        in_specs=in_specs,
        out_specs=out_specs,
        out_shape=out_shape,
        input_output_aliases=aliases,
        scratch_shapes=[pltpu.VMEM((nb, DH_M), F32)],
        compiler_params=pltpu.CompilerParams(
            dimension_semantics=("arbitrary",), vmem_limit_bytes=VMEM_LIMIT_BYTES),
        name="state_sample",
    )(*args)


def _out_sample_kernel(x_ref, hm_ref, part_ref, sgb_ref, fcache_ref, wpb_ref, wmix_ref, ln1g_ref,
                       ln1b_ref, wup_ref, wconv_ref, wdown_ref, ln2g_ref, ln2b_ref,
                       x2_ref, fnew_ref):
    yb = _dot(hm_ref[...].astype(BF16), wpb_ref[...])
    merged = part_ref[...] + sgb_ref[...] * yb
    y = ALPHA * x_ref[...] + _dot(merged.astype(BF16), wmix_ref[...])
    x1 = _layer_norm(y, ln1g_ref[...], ln1b_ref[...])
    xb = x1.astype(BF16)
    acc = jnp.zeros(x1.shape, F32)
    up_w = 2 * D_FF
    for j in range(N_FF_CHUNKS):
        convd = []
        for base in (0, D_FF):
            cols = slice(base + j * FF_CHUNK, base + (j + 1) * FF_CHUNK)
            nxt = slice(up_w + cols.start, up_w + cols.stop)
            val = _dot(xb, wup_ref[:, cols])
            row0 = fcache_ref[:, cols]
            row1 = fcache_ref[:, nxt]
            wc = wconv_ref[:, cols]
            convd.append(row0 * wc[0:1, :] + row1 * wc[1:2, :] + val * wc[2:3, :])
            fnew_ref[:, cols] = row1
            fnew_ref[:, nxt] = val
        g, u = convd
        acc = acc + _dot((g * _sigmoid(g) * u).astype(BF16),
                         wdown_ref[j * FF_CHUNK:(j + 1) * FF_CHUNK, :])
    y2 = ALPHA * x1 + acc
    x2_ref[...] = _layer_norm(y2, ln2g_ref[...], ln2b_ref[...])


def _out_sample(x, hm, part, sgb, fcache, layer, w):
    ns = x.shape[0]
    whole = lambda shape: pl.BlockSpec(shape, lambda i: (0,) * len(shape))
    act = whole((ns, D_MODEL))
    fc = whole((ns, 2 * 2 * D_FF))
    names = ("wpb", "wmix", "ln1g", "ln1b", "wup", "wfconv", "wdown", "ln2g", "ln2b")
    in_specs = [act, act, act, act, fc] + [_resident(w[n].shape, layer) for n in names]
    return pl.pallas_call(
        _out_sample_kernel,
        grid=(1,),
        in_specs=in_specs,
        out_specs=(act, fc),
        out_shape=(jax.ShapeDtypeStruct((ns, D_MODEL), F32),
                   jax.ShapeDtypeStruct((ns, 2 * 2 * D_FF), F32)),
        compiler_params=pltpu.CompilerParams(
            dimension_semantics=("arbitrary",), vmem_limit_bytes=VMEM_LIMIT_BYTES),
        name="out_sample",
    )(x, hm, part, sgb, fcache, *[w[n] for n in names])


def _pack_weights(w_in, b_igate, b_fgate, w_conv_mix, mhln_g, w_proj_a, w_proj_b, w_mix_out,
                  ln1_g, ln1_b, w_ffn_up, w_ffn_conv, w_ffn_down, ln2_g, ln2_b):
    depth = w_in.shape[0]
    o_a = 3 * D_CONV
    o_gate = o_a + 4 * D_M
    o_g = o_gate + 2 * H_M
    wgate = w_in[:, :, o_gate:o_g]
    gbias = jnp.concatenate([b_igate, b_fgate], axis=-1)
    wgc = jnp.pad(wgate, ((0, 0), (0, 0), (0, LANES - 2 * H_M)))
    gbias_row = jnp.pad(gbias, ((0, 0), (0, LANES - 2 * H_M)))[:, None, :]

    row = lambda a: a[:, None, :]
    return {
        "wa": w_in[:, :, 0:o_a].astype(BF16),
        "wqkvo": w_in[:, :, o_a:o_gate].astype(BF16),
        "wgt": wgate.transpose(0, 2, 1).astype(BF16),
        "gbias": gbias[:, :, None],
        "wgc": wgc.astype(BF16),
        "gbias_row": gbias_row,
        "wg": w_in[:, :, o_g:o_g + 2 * D_MODEL].astype(BF16),
        "wconv": w_conv_mix,
        "mhg": row(mhln_g),
        "wpa": w_proj_a.astype(BF16),
        "wpb": w_proj_b.astype(BF16),
        "wmix": w_mix_out.astype(BF16),
        "ln1g": row(ln1_g), "ln1b": row(ln1_b),
        "wup": w_ffn_up.astype(BF16),
        "wfconv": w_ffn_conv,
        "wdown": w_ffn_down.astype(BF16),
        "ln2g": row(ln2_g), "ln2b": row(ln2_b),
    }


def kernel(x_prompt, x_sample, cache_sconv, state_mlstm_C, state_mlstm_n, state_mlstm_m, cache_ffn_conv, w_in, b_igate, b_fgate, w_conv_mix, mhln_g, w_proj_a, w_proj_b, w_mix_out, ln1_g, ln1_b, w_ffn_up, w_ffn_conv, w_ffn_down, ln2_g, ln2_b):
    depth = w_in.shape[0]
    ns = x_sample.shape[0]
    assert x_sample.shape[1] == 1
    assert x_prompt.shape[1] % CHUNK == 0
    w = _pack_weights(w_in, b_igate, b_fgate, w_conv_mix, mhln_g, w_proj_a, w_proj_b, w_mix_out,
                      ln1_g, ln1_b, w_ffn_up, w_ffn_conv, w_ffn_down, ln2_g, ln2_b)
    sconv_cache = cache_sconv.reshape(depth, ns, (CONV_W - 1) * D_CONV)
    ffn_cache = cache_ffn_conv.reshape(depth, ns, (CONV_W - 1) * 2 * D_FF)

    xp = x_prompt
    xs = x_sample.reshape(ns, D_MODEL)
    outs = [[] for _ in range(9)]
    c_sample = None
    for l in range(depth):
        xp, sp, cp, np_, mp = _mixer_prompt(xp, l, w)
        xp, fp = _ffn_prompt(xp, l, w)

        q, k, v, so, gates, part, sgb, ss = _proj_sample(xs, sconv_cache[l], l, w)
        hm, c_sample, ns_, ms = _state_sample(q, k, v, so, gates, state_mlstm_m, state_mlstm_n,
                                              state_mlstm_C, l, w, c_sample)
        xs, fs = _out_sample(xs, hm, part, sgb, ffn_cache[l], l, w)

        for lst, val in zip(outs, (sp, ss.reshape(ns, CONV_W - 1, D_CONV), cp, np_, ns_, mp, ms,
                                   fp, fs.reshape(ns, CONV_W - 1, 2 * D_FF))):
            lst.append(val)

    sp, ss, cp, np_, ns_, mp, ms, fp, fs = (jnp.stack(o) for o in outs)
    return (xp, xs.reshape(ns, 1, D_MODEL), sp, ss, cp, c_sample, np_, ns_, mp, ms, fp, fs)
```

```python
import functools

import jax
import jax.numpy as jnp
from jax import lax
from jax.experimental import pallas as pl
from jax.experimental.pallas import tpu as pltpu

F32 = jnp.float32
BF16 = jnp.bfloat16

D_MODEL = 1024
D_CONV = 512
H_M = 4
DH_M = 256
D_M = H_M * DH_M
CONV_W = 3
D_FF = 2816
EPS = 1e-5
DEPTH = 4
ALPHA = (2.0 * DEPTH) ** 0.25
K_SCALE = DH_M ** -0.5

SUBLANES = 8
LANES = 128
VMEM_LIMIT_BYTES = 56 * 1024 * 1024

CHUNK = 128
MIX_CHUNK = 256
TILE_T = 512
FF_CHUNK = 256
N_FF_CHUNKS = D_FF // FF_CHUNK
FFN_RING = 2
FFN_ROW_BLOCKS = 2
SEQ_BLOCK = 8
HALO_ROW = SUBLANES - (CONV_W - 1)


def _dot(a, b):
    return jnp.dot(a, b, preferred_element_type=F32)


def _dot_nt(a, b):
    return lax.dot_general(a, b, (((1,), (1,)), ((), ())), preferred_element_type=F32)


def _dot_tn(a, b):
    return lax.dot_general(a, b, (((0,), (0,)), ((), ())), preferred_element_type=F32)


def _sigmoid(x):
    return 1.0 / (1.0 + jnp.exp(-x))


def _log_sigmoid(x):
    return -(jnp.maximum(-x, 0.0) + jnp.log1p(jnp.exp(-jnp.abs(x))))


def _layer_norm(y, g, b):
    mu = jnp.mean(y, axis=-1, keepdims=True)
    yc = y - mu
    var = jnp.mean(yc * yc, axis=-1, keepdims=True)
    return yc * lax.rsqrt(var + EPS) * g + b


def _head_norm(h):
    mu = jnp.mean(h, axis=-1, keepdims=True)
    hc = h - mu
    var = jnp.mean(hc * hc, axis=-1, keepdims=True)
    return hc * lax.rsqrt(var + EPS)


def _resident(shape, layer):
    nd = len(shape)
    return pl.BlockSpec((None,) + tuple(shape[1:]),
                        lambda *_: (layer,) + (0,) * (nd - 1),
                        pipeline_mode=pl.Buffered(1))


def _mixer_prompt_kernel_v1(x_ref, wa_ref, wqkvo_ref, wgt_ref, gbias_ref, wg_ref, wconv_ref, mhg_ref,
                         wpa_ref, wpb_ref, wmix_ref, lng_ref, lnb_ref,
                         x1_ref, sconv_ref, c_ref, n_ref, m_ref,
                         xb_ref, pbuf_ref, merged_ref, zq_ref, hm_ref, srow_ref, scol_ref):
    t = pl.program_id(1)
    tile = x_ref.shape[0]
    n_chunks = tile // CHUNK

    @pl.when(t == 0)
    def _init():
        pbuf_ref[0:SUBLANES, :] = jnp.zeros((SUBLANES, D_CONV), F32)
        c_ref[...] = jnp.zeros(c_ref.shape, F32)
        n_ref[...] = jnp.zeros(n_ref.shape, F32)
        m_ref[...] = jnp.zeros(m_ref.shape, F32)

    x = x_ref[...]
    xb_ref[...] = x.astype(BF16)
    xb = xb_ref[...]

    za = _dot(xb, wa_ref[...])
    bg = za[:, 0:D_CONV]
    p = za[:, D_CONV:2 * D_CONV] * za[:, 2 * D_CONV:3 * D_CONV]
    pbuf_ref[SUBLANES:SUBLANES + tile, :] = p
    wc = wconv_ref[...]
    conv = (pbuf_ref[HALO_ROW:HALO_ROW + tile, :] * wc[0:1, :]
            + pbuf_ref[HALO_ROW + 1:HALO_ROW + 1 + tile, :] * wc[1:2, :]
            + p * wc[2:3, :])
    halo = pbuf_ref[tile:tile + SUBLANES, :]
    pbuf_ref[0:SUBLANES, :] = halo
    sconv_ref[...] = halo
    ya = _dot((bg * conv).astype(BF16), wpa_ref[...])
    ga = _dot(xb, wg_ref[:, 0:D_MODEL])
    merged_ref[...] = _sigmoid(ga) * ya

    gates = _dot_nt(wgt_ref[...], xb) + gbias_ref[...]
    row_id = lax.broadcasted_iota(jnp.int32, (2 * H_M, CHUNK), 0)
    lane_id = lax.broadcasted_iota(jnp.int32, (2 * H_M, CHUNK), 1)
    for c in range(n_chunks):
        g = gates[:, c * CHUNK:(c + 1) * CHUNK]
        lf = _log_sigmoid(g)
        csum = lf
        shift = 1
        while shift < CHUNK:
            csum = csum + jnp.where(lane_id >= shift, pltpu.roll(csum, shift, 1), 0.0)
            shift *= 2
        srow_ref[c] = jnp.where(row_id < H_M, g, csum)
    for c in range(n_chunks):
        padded = jnp.concatenate(
            [srow_ref[c], jnp.zeros((LANES - 2 * H_M, CHUNK), F32)], axis=0)
        scol_ref[c * CHUNK:(c + 1) * CHUNK, :] = padded.T

    tri = (lax.broadcasted_iota(jnp.int32, (CHUNK, CHUNK), 0)
           >= lax.broadcasted_iota(jnp.int32, (CHUNK, CHUNK), 1))

    for h in range(H_M):
        for i in range(4):
            zq_ref[:, i * DH_M:(i + 1) * DH_M] = _dot(
                xb_ref[...], wqkvo_ref[:, i * D_M + h * DH_M:i * D_M + (h + 1) * DH_M])
        mhg =mhg_ref[:, h * DH_M:(h + 1) * DH_M]

        def chunk_body(c, carry, h=h, mhg=mhg):
            r0 = pl.multiple_of(c * CHUNK, CHUNK)
            rows = pl.ds(r0, CHUNK)
            q = zq_ref[rows, 0:DH_M]
            k = zq_ref[rows, DH_M:2 * DH_M] * K_SCALE
            v = zq_ref[rows, 2 * DH_M:3 * DH_M]
            o = zq_ref[rows, 3 * DH_M:4 * DH_M]
            qb = q.astype(BF16)
            kb = k.astype(BF16)
            vb = v.astype(BF16)
            srow = srow_ref[c]
            li_row = srow[h:h + 1, :]
            b_row = srow[H_M + h:H_M + h + 1, :]
            scol = scol_ref[rows, :]
            li_col = scol[:, h:h + 1]
            b_col = scol[:, H_M + h:H_M + h + 1]
            m0 = m_ref[h:h + 1, 0:1]
            c0 = c_ref[h]
            n0 = n_ref[h:h + 1, :]

            d = jnp.where(tri, b_col - b_row + li_row, -jnp.inf)
            m_inter = b_col + m0
            m_t = jnp.maximum(m_inter, jnp.max(d, axis=-1, keepdims=True))
            s = _dot_nt(qb, kb) * jnp.exp(d - m_t)
            scale_inter = jnp.exp(m_inter - m_t)
            numer = _dot(s.astype(BF16), vb) + scale_inter * _dot(qb, c0.astype(BF16))
            den = (jnp.sum(s, axis=-1, keepdims=True)
                   + scale_inter * jnp.sum(q * n0, axis=-1, keepdims=True))
            hc = numer / jnp.maximum(jnp.abs(den), jnp.exp(-m_t))

            b_last = b_row[:, CHUNK - 1:CHUNK]
            w_row = b_last - b_row + li_row
            m_new = jnp.maximum(b_last + m0, jnp.max(w_row, axis=-1, keepdims=True))
            decay = jnp.exp(b_last + m0 - m_new)
            w_col = jnp.exp(b_last - b_col + li_col - m_new)
            kw = k * w_col
            c_ref[h] = decay * c0 + _dot_tn(kw.astype(BF16), vb)
            n_ref[h:h + 1, :] = decay * n0 + jnp.sum(kw, axis=0, keepdims=True)
            m_ref[h:h + 1, :] = jnp.broadcast_to(m_new, (1, LANES))

            hm = _sigmoid(o) * (_head_norm(hc) * mhg)
            hm_ref[rows, h * DH_M:(h + 1) * DH_M] = hm.astype(BF16)
            return carry

        lax.fori_loop(0, n_chunks, chunk_body, 0)

    yb = _dot(hm_ref[...], wpb_ref[...])
    gb = _dot(xb_ref[...], wg_ref[:, D_MODEL:2 * D_MODEL])
    merged = merged_ref[...] + _sigmoid(gb) * yb
    y = ALPHA * x_ref[...] + _dot(merged.astype(BF16), wmix_ref[...])
    x1_ref[...] = _layer_norm(y, lng_ref[...], lnb_ref[...])


def _mixer_prompt_v1(x, layer, w):
    bsz, seq, _ = x.shape
    tile = min(TILE_T, seq)
    n_tiles = seq // tile
    n_chunks = tile // CHUNK
    row_block = lambda b, t: (b, t, 0)
    per_seq = lambda b, t: (b, 0, 0)
    in_specs = [
        pl.BlockSpec((None, tile, D_MODEL), row_block),
        _resident(w["wa"].shape, layer),
        _resident(w["wqkvo"].shape, layer),
        _resident(w["wgt"].shape, layer),
        _resident(w["gbias"].shape, layer),
        _resident(w["wg"].shape, layer),
        _resident(w["wconv"].shape, layer),
        _resident(w["mhg"].shape, layer),
        _resident(w["wpa"].shape, layer),
        _resident(w["wpb"].shape, layer),
        _resident(w["wmix"].shape, layer),
        _resident(w["ln1g"].shape, layer),
        _resident(w["ln1b"].shape, layer),
    ]
    out_shape = (
        jax.ShapeDtypeStruct((bsz, seq, D_MODEL), F32),
        jax.ShapeDtypeStruct((bsz, SUBLANES, D_CONV), F32),
        jax.ShapeDtypeStruct((bsz, H_M, DH_M, DH_M), F32),
        jax.ShapeDtypeStruct((bsz, H_M, DH_M), F32),
        jax.ShapeDtypeStruct((bsz, SUBLANES, LANES), F32),
    )
    out_specs = (
        pl.BlockSpec((None, tile, D_MODEL), row_block),
        pl.BlockSpec((None, SUBLANES, D_CONV), per_seq),
        pl.BlockSpec((None, H_M, DH_M, DH_M), lambda b, t: (b, 0, 0, 0)),
        pl.BlockSpec((None, H_M, DH_M), per_seq),
        pl.BlockSpec((None, SUBLANES, LANES), per_seq),
    )
    scratch = [
        pltpu.VMEM((tile, D_MODEL), BF16),
        pltpu.VMEM((tile + SUBLANES, D_CONV), F32),
        pltpu.VMEM((tile, D_MODEL), F32),
        pltpu.VMEM((tile, 4 * DH_M), F32),
        pltpu.VMEM((tile, D_M), BF16),
        pltpu.VMEM((n_chunks, 2 * H_M, CHUNK), F32),
        pltpu.VMEM((tile, LANES), F32),
    ]
    x1, sconv, c, n, m = pl.pallas_call(
        _mixer_prompt_kernel,
        grid=(bsz, n_tiles),
        in_specs=in_specs,
        out_specs=out_specs,
        out_shape=out_shape,
        scratch_shapes=scratch,
        compiler_params=pltpu.CompilerParams(
            dimension_semantics=("arbitrary", "arbitrary"),
            vmem_limit_bytes=VMEM_LIMIT_BYTES),
        name="mixer_prompt",
    )(x, w["wa"], w["wqkvo"], w["wgt"], w["gbias"], w["wg"], w["wconv"], w["mhg"],
      w["wpa"], w["wpb"], w["wmix"], w["ln1g"], w["ln1b"])
    return x1, sconv[:, HALO_ROW:, :], c, n, m[:, :H_M, 0]


def _lane_scan(x, op, fill, lane_id):
    shift = 1
    while shift < x.shape[1]:
        x = op(x, jnp.where(lane_id >= shift, pltpu.roll(x, shift, 1), fill))
        shift *= 2
    return x


def _mixer_prompt_kernel(x_ref, wa_ref, wqkvo_ref, wkt_ref, wgt_ref, gbias_ref, wg_ref, wconv_ref,
                         mhg_ref, wpa_ref, wpb_ref, wmix_ref, lng_ref, lnb_ref,
                         x1_ref, sconv_ref, c_ref, n_ref, m_ref,
                         xb_ref, pbuf_ref, merged_ref, q_ref, kt_ref, va_ref, o_ref, hm_ref,
                         caug_ref, mlane_ref, r_ref, w_ref, scol_ref, colmx_ref, colsc_ref,
                         coleinv_ref):
    t = pl.program_id(1)
    tile = x_ref.shape[0]
    L = MIX_CHUNK
    n_chunks = tile // L

    @pl.when(t == 0)
    def _init():
        pbuf_ref[0:SUBLANES, :] = jnp.zeros((SUBLANES, D_CONV), F32)
        caug_ref[...] = jnp.zeros(caug_ref.shape, F32)
        m_ref[...] = jnp.zeros(m_ref.shape, F32)
        mlane_ref[...] = jnp.zeros(mlane_ref.shape, F32)

    x = x_ref[...]
    xb_ref[...] = x.astype(BF16)
    xb = xb_ref[...]

    za = _dot(xb, wa_ref[...])
    bg = za[:, 0:D_CONV]
    p = za[:, D_CONV:2 * D_CONV] * za[:, 2 * D_CONV:3 * D_CONV]
    pbuf_ref[SUBLANES:SUBLANES + tile, :] = p
    wc = wconv_ref[...]
    conv = (pbuf_ref[HALO_ROW:HALO_ROW + tile, :] * wc[0:1, :]
            + pbuf_ref[HALO_ROW + 1:HALO_ROW + 1 + tile, :] * wc[1:2, :]
            + p * wc[2:3, :])
    halo = pbuf_ref[tile:tile + SUBLANES, :]
    pbuf_ref[0:SUBLANES, :] = halo
    sconv_ref[...] = halo
    ya = _dot((bg * conv).astype(BF16), wpa_ref[...])
    ga = _dot(xb, wg_ref[:, 0:D_MODEL])
    merged_ref[...] = _sigmoid(ga) * ya

    gates = _dot_nt(wgt_ref[...], xb) + gbias_ref[...]
    va_ref[:, :, DH_M:DH_M + LANES] = jnp.ones((H_M, tile, LANES), BF16)
    for h in range(H_M):
        hc_ = slice(h * DH_M, (h + 1) * DH_M)
        q_ref[:, hc_] = _dot(xb_ref[...], wqkvo_ref[:, hc_]).astype(BF16)
        kt_ref[h] = _dot_nt(wkt_ref[hc_, :], xb_ref[...]) * K_SCALE
        va_ref[h, :, 0:DH_M] = _dot(
            xb_ref[...], wqkvo_ref[:, 2 * D_M + h * DH_M:2 * D_M + (h + 1) * DH_M]).astype(BF16)
        o_ref[:, hc_] = _dot(
            xb_ref[...], wqkvo_ref[:, 3 * D_M + h * DH_M:3 * D_M + (h + 1) * DH_M])

    row_id = lax.broadcasted_iota(jnp.int32, (2 * H_M, L), 0)
    lane_id = lax.broadcasted_iota(jnp.int32, (2 * H_M, L), 1)
    head_rows = lax.broadcasted_iota(jnp.int32, (2 * H_M, 1), 0) < H_M
    m8 = m_ref[:, 0:1]
    mlane = mlane_ref[0:1, :]
    decays = []
    for c in range(n_chunks):
        g = gates[:, c * L:(c + 1) * L]
        csum = _lane_scan(_log_sigmoid(g), jnp.add, 0.0, lane_id)
        b4 = pltpu.roll(csum, H_M, 0)
        r = g - b4
        gmax = _lane_scan(r, jnp.maximum, -jnp.inf, lane_id)
        comb = jnp.where(row_id < H_M, gmax, csum)
        scol = jnp.concatenate([comb, jnp.zeros((LANES - 2 * H_M, L), F32)], axis=0).T
        scol_ref[c * L:(c + 1) * L, :] = scol

        mx_last = jnp.maximum(m8, gmax[:, L - 1:L])
        decays.append(jnp.exp(m8 - mx_last))
        r_ref[c] = r
        w_ref[c] = jnp.exp(r - mx_last)
        m8 = jnp.where(head_rows, b4[:, L - 1:L] + mx_last, 0.0)

        sc = scol_ref[c * L:(c + 1) * L, :]
        mxc = jnp.maximum(mlane, sc)
        colmx_ref[c * L:(c + 1) * L, :] = mxc
        colsc_ref[c * L:(c + 1) * L, :] = jnp.exp(mlane - mxc)
        coleinv_ref[c * L:(c + 1) * L, :] = jnp.exp(-(pltpu.roll(sc, LANES - H_M, 1) + mxc))
        last = sc[L - 1:L, :]
        mlane = pltpu.roll(last, LANES - H_M, 1) + jnp.maximum(mlane, last)
    m_ref[...] = jnp.broadcast_to(m8, m_ref.shape)
    mlane_ref[...] = jnp.broadcast_to(mlane, mlane_ref.shape)

    tri = (lax.broadcasted_iota(jnp.int32, (L, L), 0)
           >= lax.broadcasted_iota(jnp.int32, (L, L), 1))

    for c in range(n_chunks):
        rows = slice(c * L, (c + 1) * L)
        for h in range(H_M):
            hc_ = slice(h * DH_M, (h + 1) * DH_M)
            q = q_ref[rows, hc_]
            kt = kt_ref[h, :, rows]
            va = va_ref[h, rows, :]
            r_row = r_ref[c, h:h + 1, :]
            w_row = w_ref[c, h:h + 1, :]
            mx_col = colmx_ref[rows, h:h + 1]
            sc_col = colsc_ref[rows, h:h + 1]
            einv_col = coleinv_ref[rows, h:h + 1]
            decay = decays[c][h:h + 1, :]

            pmat = jnp.where(tri, jnp.exp(r_row - mx_col), 0.0)
            s = (_dot(q, kt.astype(BF16)) * pmat).astype(BF16)
            ca = caug_ref[h]
            tot = _dot(s, va) + sc_col * _dot(q, ca.astype(BF16))
            dd = jnp.maximum(jnp.abs(tot[:, DH_M:DH_M + LANES]), einv_col)
            hcell = tot[:, 0:DH_M] / jnp.concatenate([dd, dd], axis=1)
            hm = _sigmoid(o_ref[rows, hc_]) * (_head_norm(hcell) * mhg_ref[:, hc_])
            hm_ref[rows, hc_] = hm.astype(BF16)

            kw = (kt * w_row).astype(BF16)
            caug_ref[h] = decay * ca + _dot(kw, va)

    @pl.when(t == pl.num_programs(1) - 1)
    def _emit_state():
        c_ref[...] = caug_ref[:, :, 0:DH_M]
        n_ref[...] = caug_ref[:, :, DH_M:DH_M + LANES]

    yb = _dot(hm_ref[...], wpb_ref[...])
    gb = _dot(xb_ref[...], wg_ref[:, D_MODEL:2 * D_MODEL])
    merged = merged_ref[...] + _sigmoid(gb) * yb
    y = ALPHA * x_ref[...] + _dot(merged.astype(BF16), wmix_ref[...])
    x1_ref[...] = _layer_norm(y, lng_ref[...], lnb_ref[...])


def _mixer_prompt(x, layer, w):
    bsz, seq, _ = x.shape
    tile = min(TILE_T, seq)
    n_tiles = seq // tile
    n_chunks = tile // MIX_CHUNK
    row_block = lambda b, t: (b, t, 0)
    per_seq = lambda b, t: (b, 0, 0)
    per_seq4 = lambda b, t: (b, 0, 0, 0)
    names = ("wa", "wqkvo", "wkt", "wgt", "gbias", "wg", "wconv", "mhg", "wpa", "wpb", "wmix",
             "ln1g", "ln1b")
    in_specs = [pl.BlockSpec((None, tile, D_MODEL), row_block)]
    in_specs += [_resident(w[n].shape, layer) for n in names]
    out_shape = (
        jax.ShapeDtypeStruct((bsz, seq, D_MODEL), F32),
        jax.ShapeDtypeStruct((bsz, SUBLANES, D_CONV), F32),
        jax.ShapeDtypeStruct((bsz, H_M, DH_M, DH_M), F32),
        jax.ShapeDtypeStruct((bsz, H_M, DH_M, LANES), F32),
        jax.ShapeDtypeStruct((bsz, SUBLANES, LANES), F32),
    )
    out_specs = (
        pl.BlockSpec((None, tile, D_MODEL), row_block),
        pl.BlockSpec((None, SUBLANES, D_CONV), per_seq),
        pl.BlockSpec((None, H_M, DH_M, DH_M), per_seq4),
        pl.BlockSpec((None, H_M, DH_M, LANES), per_seq4),
        pl.BlockSpec((None, SUBLANES, LANES), per_seq),
    )
    col = pltpu.VMEM((tile, LANES), F32)
    scratch = [
        pltpu.VMEM((tile, D_MODEL), BF16),
        pltpu.VMEM((tile + SUBLANES, D_CONV), F32),
        pltpu.VMEM((tile, D_MODEL), F32),
        pltpu.VMEM((tile, D_M), BF16),
        pltpu.VMEM((H_M, DH_M, tile), F32),
        pltpu.VMEM((H_M, tile, DH_M + LANES), BF16),
        pltpu.VMEM((tile, D_M), F32),
        pltpu.VMEM((tile, D_M), BF16),
        pltpu.VMEM((H_M, DH_M, DH_M + LANES), F32),
        pltpu.VMEM((SUBLANES, LANES), F32),
        pltpu.VMEM((n_chunks, 2 * H_M, MIX_CHUNK), F32),
        pltpu.VMEM((n_chunks, 2 * H_M, MIX_CHUNK), F32),
        col, col, col, col,
    ]
    x1, sconv, c, n, m = pl.pallas_call(
        _mixer_prompt_kernel,
        grid=(bsz, n_tiles),
        in_specs=in_specs,
        out_specs=out_specs,
        out_shape=out_shape,
        scratch_shapes=scratch,
        compiler_params=pltpu.CompilerParams(
            dimension_semantics=("arbitrary", "arbitrary"),
            vmem_limit_bytes=VMEM_LIMIT_BYTES),
        name="mixer_prompt",
    )(x, *[w[n] for n in names])
    return x1, sconv[:, HALO_ROW:, :], c, n[:, :, :, 0], m[:, :H_M, 0]


def _ffn_prompt_kernel(x_ref, wup_ref, wconv_ref, wdown_ref, lng_ref, lnb_ref,
                       x2_ref, halo_ref,
                       xb_ref, acc_ref, *bufs):
    t = pl.program_id(1)
    tile = x_ref.shape[0]

    @pl.when(t == 0)
    def _init():
        halo_ref[...] = jnp.zeros(halo_ref.shape, F32)

    xb_ref[...] = x_ref[...].astype(BF16)

    rb = tile // FFN_ROW_BLOCKS

    def up_project(buf_ref, cols, r):
        val = _dot(xb_ref[r * rb:(r + 1) * rb, :], wup_ref[:, cols])
        if r == 0:
            buf_ref[0:SUBLANES, :] = halo_ref[:, cols]
        buf_ref[SUBLANES + r * rb:SUBLANES + (r + 1) * rb, :] = val
        if r == FFN_ROW_BLOCKS - 1:
            halo_ref[:, cols] = val[rb - SUBLANES:rb, :]

    def conv(buf_ref, cols, r):
        wc = wconv_ref[:, cols]
        lo = r * rb
        return (buf_ref[HALO_ROW + lo:HALO_ROW + lo + rb, :] * wc[0:1, :]
                + buf_ref[HALO_ROW + 1 + lo:HALO_ROW + 1 + lo + rb, :] * wc[1:2, :]
                + buf_ref[SUBLANES + lo:SUBLANES + lo + rb, :] * wc[2:3, :])

    n_ring = len(bufs) // 2
    g_cols = lambda j: slice(j * FF_CHUNK, (j + 1) * FF_CHUNK)
    u_cols = lambda j: slice(D_FF + j * FF_CHUNK, D_FF + (j + 1) * FF_CHUNK)
    ring = lambda j: (bufs[2 * (j % n_ring)], bufs[2 * (j % n_ring) + 1])

    def stage(j, r):
        gbuf, ubuf = ring(j)
        up_project(gbuf, g_cols(j), r)
        up_project(ubuf, u_cols(j), r)

    for r in range(FFN_ROW_BLOCKS):
        stage(0, r)
    for j in range(N_FF_CHUNKS):
        gbuf, ubuf = ring(j)
        for r in range(FFN_ROW_BLOCKS):
            rows = slice(r * rb, (r + 1) * rb)
            g = conv(gbuf, g_cols(j), r)
            u = conv(ubuf, u_cols(j), r)
            hdn = (g * _sigmoid(g) * u).astype(BF16)
            if j + 1 < N_FF_CHUNKS:
                stage(j + 1, r)
            part = _dot(hdn, wdown_ref[j * FF_CHUNK:(j + 1) * FF_CHUNK, :])
            if j == 0:
                acc_ref[rows, :] = part
            else:
                acc_ref[rows, :] += part

    y = ALPHA * x_ref[...] + acc_ref[...]
    x2_ref[...] = _layer_norm(y, lng_ref[...], lnb_ref[...])


def _ffn_prompt(x, layer, w):
    bsz, seq, _ = x.shape
    tile = min(TILE_T, seq)
    n_tiles = seq // tile
    row_block = lambda b, t: (b, t, 0)
    in_specs = [
        pl.BlockSpec((None, tile, D_MODEL), row_block),
        _resident(w["wup"].shape, layer),
        _resident(w["wfconv"].shape, layer),
        _resident(w["wdown"].shape, layer),
        _resident(w["ln2g"].shape, layer),
        _resident(w["ln2b"].shape, layer),
    ]
    x2, halo = pl.pallas_call(
        _ffn_prompt_kernel,
        grid=(bsz, n_tiles),
        in_specs=in_specs,
        out_specs=(pl.BlockSpec((None, tile, D_MODEL), row_block),
                   pl.BlockSpec((None, SUBLANES, 2 * D_FF), lambda b, t: (b, 0, 0))),
        out_shape=(jax.ShapeDtypeStruct((bsz, seq, D_MODEL), F32),
                   jax.ShapeDtypeStruct((bsz, SUBLANES, 2 * D_FF), F32)),
        scratch_shapes=[
            pltpu.VMEM((tile, D_MODEL), BF16),
            pltpu.VMEM((tile, D_MODEL), F32),
        ] + [pltpu.VMEM((tile + SUBLANES, FF_CHUNK), F32)] * (2 * FFN_RING),
        compiler_params=pltpu.CompilerParams(
            dimension_semantics=("arbitrary", "arbitrary"),
            vmem_limit_bytes=VMEM_LIMIT_BYTES),
        name="ffn_prompt",
    )(x, w["wup"], w["wfconv"], w["wdown"], w["ln2g"], w["ln2b"])
    return x2, halo[:, HALO_ROW:, :]


def _proj_sample_kernel(x_ref, cache_ref, wa_ref, wqkvo_ref, wgc_ref, gbias_ref, wg_ref, wconv_ref,
                        wpa_ref,
                        q_ref, k_ref, v_ref, so_ref, gates_ref, part_ref, sgb_ref, sconv_ref):
    xb = x_ref[...].astype(BF16)
    za = _dot(xb, wa_ref[...])
    bg = za[:, 0:D_CONV]
    p = za[:, D_CONV:2 * D_CONV] * za[:, 2 * D_CONV:3 * D_CONV]
    cache0 = cache_ref[:, 0:D_CONV]
    cache1 = cache_ref[:, D_CONV:2 * D_CONV]
    wc = wconv_ref[...]
    conv = cache0 * wc[0:1, :] + cache1 * wc[1:2, :] + p * wc[2:3, :]
    sconv_ref[:, 0:D_CONV] = cache1
    sconv_ref[:, D_CONV:2 * D_CONV] = p
    ya = _dot((bg * conv).astype(BF16), wpa_ref[...])
    zg = _dot(xb, wg_ref[...])
    part_ref[...] = _sigmoid(zg[:, 0:D_MODEL]) * ya
    sgb_ref[...] = _sigmoid(zg[:, D_MODEL:2 * D_MODEL])
    gz = _dot(xb, wgc_ref[...]) + gbias_ref[...]
    lane = lax.broadcasted_iota(jnp.int32, gz.shape, 1)
    gates_ref[...] = jnp.where(lane < H_M, gz, _log_sigmoid(gz))
    q_ref[...] = _dot(xb, wqkvo_ref[:, 0:D_M])
    k_ref[...] = _dot(xb, wqkvo_ref[:, D_M:2 * D_M]) * K_SCALE
    v_ref[...] = _dot(xb, wqkvo_ref[:, 2 * D_M:3 * D_M])
    so_ref[...] = _sigmoid(_dot(xb, wqkvo_ref[:, 3 * D_M:4 * D_M]))


def _proj_sample(x, cache, layer, w):
    ns = x.shape[0]
    whole = lambda shape: pl.BlockSpec(shape, lambda i: (0,) * len(shape))
    act = jax.ShapeDtypeStruct((ns, D_MODEL), F32)
    in_specs = [
        whole((ns, D_MODEL)),
        whole((ns, 2 * D_CONV)),
        _resident(w["wa"].shape, layer),
        _resident(w["wqkvo"].shape, layer),
        _resident(w["wgc"].shape, layer),
        _resident(w["gbias_row"].shape, layer),
        _resident(w["wg"].shape, layer),
        _resident(w["wconv"].shape, layer),
        _resident(w["wpa"].shape, layer),
    ]
    out_shape = (act, act, act, act, jax.ShapeDtypeStruct((ns, LANES), F32), act, act,
                 jax.ShapeDtypeStruct((ns, 2 * D_CONV), F32))
    out_specs = tuple(whole(s.shape) for s in out_shape)
    return pl.pallas_call(
        _proj_sample_kernel,
        grid=(1,),
        in_specs=in_specs,
        out_specs=out_specs,
        out_shape=out_shape,
        compiler_params=pltpu.CompilerParams(
            dimension_semantics=("arbitrary",), vmem_limit_bytes=VMEM_LIMIT_BYTES),
        name="proj_sample",
    )(x, cache, w["wa"], w["wqkvo"], w["wgc"], w["gbias_row"], w["wg"], w["wconv"], w["wpa"])


def _state_sample_kernel(q_ref, k_ref, v_ref, so_ref, gates_ref, m0_ref, n0_ref, c0_ref, mhg_ref,
                         *rest):
    hm_ref, c_ref, n_ref, m_ref, qc_ref = rest[-5:]
    nb = q_ref.shape[0]
    gates = gates_ref[...]
    zpad = jnp.zeros((LANES - nb, DH_M), F32)
    for h in range(H_M):
        cols = slice(h * DH_M, (h + 1) * DH_M)
        q = q_ref[:, cols]
        k = k_ref[:, cols]
        v = v_ref[:, cols]
        li = gates[:, h:h + 1]
        lf = gates[:, H_M + h:H_M + h + 1]
        m0 = m0_ref[:, h:h + 1]
        n0 = n0_ref[:, h, :]
        m_new = jnp.maximum(lf + m0, li)
        decay = jnp.exp(lf + m0 - m_new)
        wgt = jnp.exp(li - m_new)
        kw = k * wgt
        q_t = jnp.concatenate([q, zpad], axis=0).T
        kw_t = jnp.concatenate([kw, zpad], axis=0).T
        for i in range(nb):
            c0 = c0_ref[i, h]
            qc_ref[i:i + 1, :] = jnp.sum(q_t[:, i:i + 1] * c0, axis=0, keepdims=True)
            c_ref[i, h] = decay[i:i + 1, :] * c0 + kw_t[:, i:i + 1] * v[i:i + 1, :]
        s = jnp.sum(q * k, axis=-1, keepdims=True) * wgt
        numer = s * v + decay * qc_ref[...]
        den = s + decay * jnp.sum(q * n0, axis=-1, keepdims=True)
        hc = numer / jnp.maximum(jnp.abs(den), jnp.exp(-m_new))
        hm_ref[:, cols] = so_ref[:, cols] * (_head_norm(hc) * mhg_ref[:, cols])
        n_ref[:, h, :] = decay * n0 + kw
        m_ref[:, h:h + 1] = m_new


def _state_sample(q, k, v, so, gates, m0, n0, c0, layer, w, c_all):
    depth = c0.shape[0]
    ns = q.shape[0]
    nb = min(SEQ_BLOCK, ns)
    rows = lambda i: (i, 0)
    act_spec = pl.BlockSpec((nb, D_MODEL), rows)
    layer_slab = lambda i: (layer, i, 0, 0, 0)
    in_specs = [
        act_spec, act_spec, act_spec, act_spec,
        pl.BlockSpec((nb, LANES), rows),
        pl.BlockSpec((None, nb, H_M), lambda i: (layer, i, 0)),
        pl.BlockSpec((None, nb, H_M, DH_M), lambda i: (layer, i, 0, 0)),
        pl.BlockSpec((None, nb, H_M, DH_M, DH_M), layer_slab),
        pl.BlockSpec((None, 1, D_M), lambda i: (layer, 0, 0)),
    ]
    args = [q, k, v, so, gates, m0, n0, c0, w["mhg"]]
    aliases = {}
    if c_all is not None:
        in_specs.append(pl.BlockSpec(memory_space=pl.ANY))
        args.append(c_all)
        aliases = {len(args) - 1: 1}
    out_shape = (
        jax.ShapeDtypeStruct((ns, D_M), F32),
        jax.ShapeDtypeStruct((depth, ns, H_M, DH_M, DH_M), F32),
        jax.ShapeDtypeStruct((ns, H_M, DH_M), F32),
        jax.ShapeDtypeStruct((ns, H_M), F32),
    )
    out_specs = (
        act_spec,
        pl.BlockSpec((None, nb, H_M, DH_M, DH_M), layer_slab),
        pl.BlockSpec((nb, H_M, DH_M), lambda i: (i, 0, 0)),
        pl.BlockSpec((nb, H_M), rows),
    )
    return pl.pallas_call(
        _state_sample_kernel,
        grid=(ns // nb,),
        in_specs=in_specs,
        out_specs=out_specs,
        out_shape=out_shape,
        input_output_aliases=aliases,
        scratch_shapes=[pltpu.VMEM((nb, DH_M), F32)],
        compiler_params=pltpu.CompilerParams(
            dimension_semantics=("arbitrary",), vmem_limit_bytes=VMEM_LIMIT_BYTES),
        name="state_sample",
    )(*args)


def _out_sample_kernel(x_ref, hm_ref, part_ref, sgb_ref, fcache_ref, wpb_ref, wmix_ref, ln1g_ref,
                       ln1b_ref, wup_ref, wconv_ref, wdown_ref, ln2g_ref, ln2b_ref,
                       x2_ref, fnew_ref):
    yb = _dot(hm_ref[...].astype(BF16), wpb_ref[...])
    merged = part_ref[...] + sgb_ref[...] * yb
    y = ALPHA * x_ref[...] + _dot(merged.astype(BF16), wmix_ref[...])
    x1 = _layer_norm(y, ln1g_ref[...], ln1b_ref[...])
    xb = x1.astype(BF16)
    acc = jnp.zeros(x1.shape, F32)
    up_w = 2 * D_FF
    for j in range(N_FF_CHUNKS):
        convd = []
        for base in (0, D_FF):
            cols = slice(base + j * FF_CHUNK, base + (j + 1) * FF_CHUNK)
            nxt = slice(up_w + cols.start, up_w + cols.stop)
            val = _dot(xb, wup_ref[:, cols])
            row0 = fcache_ref[:, cols]
            row1 = fcache_ref[:, nxt]
            wc = wconv_ref[:, cols]
            convd.append(row0 * wc[0:1, :] + row1 * wc[1:2, :] + val * wc[2:3, :])
            fnew_ref[:, cols] = row1
            fnew_ref[:, nxt] = val
        g, u = convd
        acc = acc + _dot((g * _sigmoid(g) * u).astype(BF16),
                         wdown_ref[j * FF_CHUNK:(j + 1) * FF_CHUNK, :])
    y2 = ALPHA * x1 + acc
    x2_ref[...] = _layer_norm(y2, ln2g_ref[...], ln2b_ref[...])


def _out_sample(x, hm, part, sgb, fcache, layer, w):
    ns = x.shape[0]
    whole = lambda shape: pl.BlockSpec(shape, lambda i: (0,) * len(shape))
    act = whole((ns, D_MODEL))
    fc = whole((ns, 2 * 2 * D_FF))
    names = ("wpb", "wmix", "ln1g", "ln1b", "wup", "wfconv", "wdown", "ln2g", "ln2b")
    in_specs = [act, act, act, act, fc] + [_resident(w[n].shape, layer) for n in names]
    return pl.pallas_call(
        _out_sample_kernel,
        grid=(1,),
        in_specs=in_specs,
        out_specs=(act, fc),
        out_shape=(jax.ShapeDtypeStruct((ns, D_MODEL), F32),
                   jax.ShapeDtypeStruct((ns, 2 * 2 * D_FF), F32)),
        compiler_params=pltpu.CompilerParams(
            dimension_semantics=("arbitrary",), vmem_limit_bytes=VMEM_LIMIT_BYTES),
        name="out_sample",
    )(x, hm, part, sgb, fcache, *[w[n] for n in names])


def _pack_weights(w_in, b_igate, b_fgate, w_conv_mix, mhln_g, w_proj_a, w_proj_b, w_mix_out,
                  ln1_g, ln1_b, w_ffn_up, w_ffn_conv, w_ffn_down, ln2_g, ln2_b):
    depth = w_in.shape[0]
    o_a = 3 * D_CONV
    o_gate = o_a + 4 * D_M
    o_g = o_gate + 2 * H_M
    wgate = w_in[:, :, o_gate:o_g]
    gbias = jnp.concatenate([b_igate, b_fgate], axis=-1)
    wgc = jnp.pad(wgate, ((0, 0), (0, 0), (0, LANES - 2 * H_M)))
    gbias_row = jnp.pad(gbias, ((0, 0), (0, LANES - 2 * H_M)))[:, None, :]

    row = lambda a: a[:, None, :]
    return {
        "wa": w_in[:, :, 0:o_a].astype(BF16),
        "wqkvo": w_in[:, :, o_a:o_gate].astype(BF16),
        "wkt": w_in[:, :, o_a + D_M:o_a + 2 * D_M].transpose(0, 2, 1).astype(BF16),
        "wgt": wgate.transpose(0, 2, 1).astype(BF16),
        "gbias": gbias[:, :, None],
        "wgc": wgc.astype(BF16),
        "gbias_row": gbias_row,
        "wg": w_in[:, :, o_g:o_g + 2 * D_MODEL].astype(BF16),
        "wconv": w_conv_mix,
        "mhg": row(mhln_g),
        "wpa": w_proj_a.astype(BF16),
        "wpb": w_proj_b.astype(BF16),
        "wmix": w_mix_out.astype(BF16),
        "ln1g": row(ln1_g), "ln1b": row(ln1_b),
        "wup": w_ffn_up.astype(BF16),
        "wfconv": w_ffn_conv,
        "wdown": w_ffn_down.astype(BF16),
        "ln2g": row(ln2_g), "ln2b": row(ln2_b),
    }


def kernel(x_prompt, x_sample, cache_sconv, state_mlstm_C, state_mlstm_n, state_mlstm_m, cache_ffn_conv, w_in, b_igate, b_fgate, w_conv_mix, mhln_g, w_proj_a, w_proj_b, w_mix_out, ln1_g, ln1_b, w_ffn_up, w_ffn_conv, w_ffn_down, ln2_g, ln2_b):
    depth = w_in.shape[0]
    ns = x_sample.shape[0]
    assert x_sample.shape[1] == 1
    assert x_prompt.shape[1] % CHUNK == 0
    w = _pack_weights(w_in, b_igate, b_fgate, w_conv_mix, mhln_g, w_proj_a, w_proj_b, w_mix_out,
                      ln1_g, ln1_b, w_ffn_up, w_ffn_conv, w_ffn_down, ln2_g, ln2_b)
    sconv_cache = cache_sconv.reshape(depth, ns, (CONV_W - 1) * D_CONV)
    ffn_cache = cache_ffn_conv.reshape(depth, ns, (CONV_W - 1) * 2 * D_FF)

    xp = x_prompt
    xs = x_sample.reshape(ns, D_MODEL)
    outs = [[] for _ in range(9)]
    c_sample = None
    for l in range(depth):
        xp, sp, cp, np_, mp = _mixer_prompt(xp, l, w)
        xp, fp = _ffn_prompt(xp, l, w)

        q, k, v, so, gates, part, sgb, ss = _proj_sample(xs, sconv_cache[l], l, w)
        hm, c_sample, ns_, ms = _state_sample(q, k, v, so, gates, state_mlstm_m, state_mlstm_n,
                                              state_mlstm_C, l, w, c_sample)
        xs, fs = _out_sample(xs, hm, part, sgb, ffn_cache[l], l, w)

        for lst, val in zip(outs, (sp, ss.reshape(ns, CONV_W - 1, D_CONV), cp, np_, ns_, mp, ms,
                                   fp, fs.reshape(ns, CONV_W - 1, 2 * D_FF))):
            lst.append(val)

    sp, ss, cp, np_, ns_, mp, ms, fp, fs = (jnp.stack(o) for o in outs)
    return (xp, xs.reshape(ns, 1, D_MODEL), sp, ss, cp, c_sample, np_, ns_, mp, ms, fp, fs)
```

```python
import functools

import jax
import jax.numpy as jnp
from jax import lax
from jax.experimental import pallas as pl
from jax.experimental.pallas import tpu as pltpu

F32 = jnp.float32
BF16 = jnp.bfloat16

D_MODEL = 1024
D_CONV = 512
H_M = 4
DH_M = 256
D_M = H_M * DH_M
CONV_W = 3
D_FF = 2816
EPS = 1e-5
DEPTH = 4
ALPHA = (2.0 * DEPTH) ** 0.25
K_SCALE = DH_M ** -0.5

SUBLANES = 8
LANES = 128
VMEM_LIMIT_BYTES = 56 * 1024 * 1024

CHUNK = 128
MIX_CHUNK = 256
TILE_T = 512
FF_CHUNK = 256
N_FF_CHUNKS = D_FF // FF_CHUNK
FFN_RING = 2
FFN_ROW_BLOCKS = 2
SEQ_BLOCK = 8
HALO_ROW = SUBLANES - (CONV_W - 1)


def _dot(a, b):
    return jnp.dot(a, b, preferred_element_type=F32)


def _dot_nt(a, b):
    return lax.dot_general(a, b, (((1,), (1,)), ((), ())), preferred_element_type=F32)


def _sigmoid(x):
    return 1.0 / (1.0 + jnp.exp(-x))


def _log_sigmoid(x):
    return -(jnp.maximum(-x, 0.0) + jnp.log1p(jnp.exp(-jnp.abs(x))))


def _layer_norm(y, g, b):
    mu = jnp.mean(y, axis=-1, keepdims=True)
    yc = y - mu
    var = jnp.mean(yc * yc, axis=-1, keepdims=True)
    return yc * lax.rsqrt(var + EPS) * g + b


def _head_norm(h):
    mu = jnp.mean(h, axis=-1, keepdims=True)
    hc = h - mu
    var = jnp.mean(hc * hc, axis=-1, keepdims=True)
    return hc * lax.rsqrt(var + EPS)


def _resident(shape, layer):
    nd = len(shape)
    return pl.BlockSpec((None,) + tuple(shape[1:]),
                        lambda *_: (layer,) + (0,) * (nd - 1),
                        pipeline_mode=pl.Buffered(1))


def _lane_scan(x, op, fill, lane_id):
    shift = 1
    while shift < x.shape[1]:
        x = op(x, jnp.where(lane_id >= shift, pltpu.roll(x, shift, 1), fill))
        shift *= 2
    return x


def _mixer_prompt_kernel(x_ref, wa_ref, wqkvo_ref, wkt_ref, wgt_ref, gbias_ref, wg_ref, wconv_ref,
                         mhg_ref, wpa_ref, wpb_ref, wmix_ref, lng_ref, lnb_ref,
                         x1_ref, sconv_ref, c_ref, n_ref, m_ref,
                         xb_ref, pbuf_ref, merged_ref, q_ref, kt_ref, va_ref, o_ref, hm_ref,
                         caug_ref, mlane_ref, r_ref, w_ref, scol_ref, colmx_ref, colsc_ref,
                         coleinv_ref):
    t = pl.program_id(1)
    tile = x_ref.shape[0]
    L = MIX_CHUNK
    n_chunks = tile // L

    @pl.when(t == 0)
    def _init():
        pbuf_ref[0:SUBLANES, :] = jnp.zeros((SUBLANES, D_CONV), F32)
        caug_ref[...] = jnp.zeros(caug_ref.shape, F32)
        m_ref[...] = jnp.zeros(m_ref.shape, F32)
        mlane_ref[...] = jnp.zeros(mlane_ref.shape, F32)

    x = x_ref[...]
    xb_ref[...] = x.astype(BF16)
    xb = xb_ref[...]

    za = _dot(xb, wa_ref[...])
    bg = za[:, 0:D_CONV]
    p = za[:, D_CONV:2 * D_CONV] * za[:, 2 * D_CONV:3 * D_CONV]
    pbuf_ref[SUBLANES:SUBLANES + tile, :] = p
    wc = wconv_ref[...]
    conv = (pbuf_ref[HALO_ROW:HALO_ROW + tile, :] * wc[0:1, :]
            + pbuf_ref[HALO_ROW + 1:HALO_ROW + 1 + tile, :] * wc[1:2, :]
            + p * wc[2:3, :])
    halo = pbuf_ref[tile:tile + SUBLANES, :]
    pbuf_ref[0:SUBLANES, :] = halo
    sconv_ref[...] = halo
    ya = _dot((bg * conv).astype(BF16), wpa_ref[...])
    ga = _dot(xb, wg_ref[:, 0:D_MODEL])
    merged_ref[...] = _sigmoid(ga) * ya

    gates = _dot_nt(wgt_ref[...], xb) + gbias_ref[...]
    va_ref[:, :, DH_M:DH_M + LANES] = jnp.ones((H_M, tile, LANES), BF16)
    for h in range(H_M):
        hc_ = slice(h * DH_M, (h + 1) * DH_M)
        q_ref[:, hc_] = _dot(xb_ref[...], wqkvo_ref[:, hc_]).astype(BF16)
        kt_ref[h] = _dot_nt(wkt_ref[hc_, :], xb_ref[...]) * K_SCALE
        va_ref[h, :, 0:DH_M] = _dot(
            xb_ref[...], wqkvo_ref[:, 2 * D_M + h * DH_M:2 * D_M + (h + 1) * DH_M]).astype(BF16)
        o_ref[:, hc_] = _dot(
            xb_ref[...], wqkvo_ref[:, 3 * D_M + h * DH_M:3 * D_M + (h + 1) * DH_M])

    row_id = lax.broadcasted_iota(jnp.int32, (2 * H_M, L), 0)
    lane_id = lax.broadcasted_iota(jnp.int32, (2 * H_M, L), 1)
    head_rows = lax.broadcasted_iota(jnp.int32, (2 * H_M, 1), 0) < H_M
    m8 = m_ref[:, 0:1]
    mlane = mlane_ref[0:1, :]
    decays = []
    for c in range(n_chunks):
        g = gates[:, c * L:(c + 1) * L]
        csum = _lane_scan(_log_sigmoid(g), jnp.add, 0.0, lane_id)
        b4 = pltpu.roll(csum, H_M, 0)
        r = g - b4
        gmax = _lane_scan(r, jnp.maximum, -jnp.inf, lane_id)
        comb = jnp.where(row_id < H_M, gmax, csum)
        scol = jnp.concatenate([comb, jnp.zeros((LANES - 2 * H_M, L), F32)], axis=0).T
        scol_ref[c * L:(c + 1) * L, :] = scol

        mx_last = jnp.maximum(m8, gmax[:, L - 1:L])
        decays.append(jnp.exp(m8 - mx_last))
        r_ref[c] = r
        w_ref[c] = jnp.exp(r - mx_last)
        m8 = jnp.where(head_rows, b4[:, L - 1:L] + mx_last, 0.0)

        sc = scol_ref[c * L:(c + 1) * L, :]
        mxc = jnp.maximum(mlane, sc)
        colmx_ref[c * L:(c + 1) * L, :] = mxc
        colsc_ref[c * L:(c + 1) * L, :] = jnp.exp(mlane - mxc)
        coleinv_ref[c * L:(c + 1) * L, :] = jnp.exp(-(pltpu.roll(sc, LANES - H_M, 1) + mxc))
        last = sc[L - 1:L, :]
        mlane = pltpu.roll(last, LANES - H_M, 1) + jnp.maximum(mlane, last)
    m_ref[...] = jnp.broadcast_to(m8, m_ref.shape)
    mlane_ref[...] = jnp.broadcast_to(mlane, mlane_ref.shape)

    tri = (lax.broadcasted_iota(jnp.int32, (L, L), 0)
           >= lax.broadcasted_iota(jnp.int32, (L, L), 1))

    for c in range(n_chunks):
        rows = slice(c * L, (c + 1) * L)
        for h in range(H_M):
            hc_ = slice(h * DH_M, (h + 1) * DH_M)
            q = q_ref[rows, hc_]
            kt = kt_ref[h, :, rows]
            va = va_ref[h, rows, :]
            r_row = r_ref[c, h:h + 1, :]
            w_row = w_ref[c, h:h + 1, :]
            mx_col = colmx_ref[rows, h:h + 1]
            sc_col = colsc_ref[rows, h:h + 1]
            einv_col = coleinv_ref[rows, h:h + 1]
            decay = decays[c][h:h + 1, :]

            pmat = jnp.where(tri, jnp.exp(r_row - mx_col), 0.0)
            s = (_dot(q, kt.astype(BF16)) * pmat).astype(BF16)
            ca = caug_ref[h]
            tot = _dot(s, va) + sc_col * _dot(q, ca.astype(BF16))
            dd = jnp.maximum(jnp.abs(tot[:, DH_M:DH_M + LANES]), einv_col)
            hcell = tot[:, 0:DH_M] / jnp.concatenate([dd, dd], axis=1)
            hm = _sigmoid(o_ref[rows, hc_]) * (_head_norm(hcell) * mhg_ref[:, hc_])
            hm_ref[rows, hc_] = hm.astype(BF16)

            kw = (kt * w_row).astype(BF16)
            caug_ref[h] = decay * ca + _dot(kw, va)

    @pl.when(t == pl.num_programs(1) - 1)
    def _emit_state():
        c_ref[...] = caug_ref[:, :, 0:DH_M]
        n_ref[...] = caug_ref[:, :, DH_M:DH_M + LANES]

    yb = _dot(hm_ref[...], wpb_ref[...])
    gb = _dot(xb_ref[...], wg_ref[:, D_MODEL:2 * D_MODEL])
    merged = merged_ref[...] + _sigmoid(gb) * yb
    y = ALPHA * x_ref[...] + _dot(merged.astype(BF16), wmix_ref[...])
    x1_ref[...] = _layer_norm(y, lng_ref[...], lnb_ref[...])


def _mixer_prompt(x, layer, w):
    bsz, seq, _ = x.shape
    tile = min(TILE_T, seq)
    n_tiles = seq // tile
    n_chunks = tile // MIX_CHUNK
    row_block = lambda b, t: (b, t, 0)
    per_seq = lambda b, t: (b, 0, 0)
    per_seq4 = lambda b, t: (b, 0, 0, 0)
    names = ("wa", "wqkvo", "wkt", "wgt", "gbias", "wg", "wconv", "mhg", "wpa", "wpb", "wmix",
             "ln1g", "ln1b")
    in_specs = [pl.BlockSpec((None, tile, D_MODEL), row_block)]
    in_specs += [_resident(w[n].shape, layer) for n in names]
    out_shape = (
        jax.ShapeDtypeStruct((bsz, seq, D_MODEL), F32),
        jax.ShapeDtypeStruct((bsz, SUBLANES, D_CONV), F32),
        jax.ShapeDtypeStruct((bsz, H_M, DH_M, DH_M), F32),
        jax.ShapeDtypeStruct((bsz, H_M, DH_M, LANES), F32),
        jax.ShapeDtypeStruct((bsz, SUBLANES, LANES), F32),
    )
    out_specs = (
        pl.BlockSpec((None, tile, D_MODEL), row_block),
        pl.BlockSpec((None, SUBLANES, D_CONV), per_seq),
        pl.BlockSpec((None, H_M, DH_M, DH_M), per_seq4),
        pl.BlockSpec((None, H_M, DH_M, LANES), per_seq4),
        pl.BlockSpec((None, SUBLANES, LANES), per_seq),
    )
    col = pltpu.VMEM((tile, LANES), F32)
    scratch = [
        pltpu.VMEM((tile, D_MODEL), BF16),
        pltpu.VMEM((tile + SUBLANES, D_CONV), F32),
        pltpu.VMEM((tile, D_MODEL), F32),
        pltpu.VMEM((tile, D_M), BF16),
        pltpu.VMEM((H_M, DH_M, tile), F32),
        pltpu.VMEM((H_M, tile, DH_M + LANES), BF16),
        pltpu.VMEM((tile, D_M), F32),
        pltpu.VMEM((tile, D_M), BF16),
        pltpu.VMEM((H_M, DH_M, DH_M + LANES), F32),
        pltpu.VMEM((SUBLANES, LANES), F32),
        pltpu.VMEM((n_chunks, 2 * H_M, MIX_CHUNK), F32),
        pltpu.VMEM((n_chunks, 2 * H_M, MIX_CHUNK), F32),
        col, col, col, col,
    ]
    x1, sconv, c, n, m = pl.pallas_call(
        _mixer_prompt_kernel,
        grid=(bsz, n_tiles),
        in_specs=in_specs,
        out_specs=out_specs,
        out_shape=out_shape,
        scratch_shapes=scratch,
        compiler_params=pltpu.CompilerParams(
            dimension_semantics=("arbitrary", "arbitrary"),
            vmem_limit_bytes=VMEM_LIMIT_BYTES),
        name="mixer_prompt",
    )(x, *[w[n] for n in names])
    return x1, sconv[:, HALO_ROW:, :], c, n[:, :, :, 0], m[:, :H_M, 0]


def _ffn_prompt_kernel(x_ref, wup_ref, wconv_ref, wdown_ref, lng_ref, lnb_ref,
                       x2_ref, halo_ref,
                       xb_ref, acc_ref, *bufs):
    t = pl.program_id(1)
    tile = x_ref.shape[0]

    @pl.when(t == 0)
    def _init():
        halo_ref[...] = jnp.zeros(halo_ref.shape, F32)

    xb_ref[...] = x_ref[...].astype(BF16)

    rb = tile // FFN_ROW_BLOCKS

    def up_project(buf_ref, cols, r):
        val = _dot(xb_ref[r * rb:(r + 1) * rb, :], wup_ref[:, cols])
        if r == 0:
            buf_ref[0:SUBLANES, :] = halo_ref[:, cols]
        buf_ref[SUBLANES + r * rb:SUBLANES + (r + 1) * rb, :] = val
        if r == FFN_ROW_BLOCKS - 1:
            halo_ref[:, cols] = val[rb - SUBLANES:rb, :]

    def conv(buf_ref, cols, r):
        wc = wconv_ref[:, cols]
        lo = r * rb
        return (buf_ref[HALO_ROW + lo:HALO_ROW + lo + rb, :] * wc[0:1, :]
                + buf_ref[HALO_ROW + 1 + lo:HALO_ROW + 1 + lo + rb, :] * wc[1:2, :]
                + buf_ref[SUBLANES + lo:SUBLANES + lo + rb, :] * wc[2:3, :])

    n_ring = len(bufs) // 2
    g_cols = lambda j: slice(j * FF_CHUNK, (j + 1) * FF_CHUNK)
    u_cols = lambda j: slice(D_FF + j * FF_CHUNK, D_FF + (j + 1) * FF_CHUNK)
    ring = lambda j: (bufs[2 * (j % n_ring)], bufs[2 * (j % n_ring) + 1])

    def stage(j, r):
        gbuf, ubuf = ring(j)
        up_project(gbuf, g_cols(j), r)
        up_project(ubuf, u_cols(j), r)

    for r in range(FFN_ROW_BLOCKS):
        stage(0, r)
    for j in range(N_FF_CHUNKS):
        gbuf, ubuf = ring(j)
        for r in range(FFN_ROW_BLOCKS):
            rows = slice(r * rb, (r + 1) * rb)
            g = conv(gbuf, g_cols(j), r)
            u = conv(ubuf, u_cols(j), r)
            hdn = (g * _sigmoid(g) * u).astype(BF16)
            if j + 1 < N_FF_CHUNKS:
                stage(j + 1, r)
            part = _dot(hdn, wdown_ref[j * FF_CHUNK:(j + 1) * FF_CHUNK, :])
            if j == 0:
                acc_ref[rows, :] = part
            else:
                acc_ref[rows, :] += part

    y = ALPHA * x_ref[...] + acc_ref[...]
    x2_ref[...] = _layer_norm(y, lng_ref[...], lnb_ref[...])


def _ffn_prompt(x, layer, w):
    bsz, seq, _ = x.shape
    tile = min(TILE_T, seq)
    n_tiles = seq // tile
    row_block = lambda b, t: (b, t, 0)
    in_specs = [
        pl.BlockSpec((None, tile, D_MODEL), row_block),
        _resident(w["wup"].shape, layer),
        _resident(w["wfconv"].shape, layer),
        _resident(w["wdown"].shape, layer),
        _resident(w["ln2g"].shape, layer),
        _resident(w["ln2b"].shape, layer),
    ]
    x2, halo = pl.pallas_call(
        _ffn_prompt_kernel,
        grid=(bsz, n_tiles),
        in_specs=in_specs,
        out_specs=(pl.BlockSpec((None, tile, D_MODEL), row_block),
                   pl.BlockSpec((None, SUBLANES, 2 * D_FF), lambda b, t: (b, 0, 0))),
        out_shape=(jax.ShapeDtypeStruct((bsz, seq, D_MODEL), F32),
                   jax.ShapeDtypeStruct((bsz, SUBLANES, 2 * D_FF), F32)),
        scratch_shapes=[
            pltpu.VMEM((tile, D_MODEL), BF16),
            pltpu.VMEM((tile, D_MODEL), F32),
        ] + [pltpu.VMEM((tile + SUBLANES, FF_CHUNK), F32)] * (2 * FFN_RING),
        compiler_params=pltpu.CompilerParams(
            dimension_semantics=("arbitrary", "arbitrary"),
            vmem_limit_bytes=VMEM_LIMIT_BYTES),
        name="ffn_prompt",
    )(x, w["wup"], w["wfconv"], w["wdown"], w["ln2g"], w["ln2b"])
    return x2, halo[:, HALO_ROW:, :]


def _proj_sample_kernel(x_ref, cache_ref, wa_ref, wqkvo_ref, wgc_ref, gbias_ref, wg_ref, wconv_ref,
                        wpa_ref,
                        q_ref, k_ref, v_ref, so_ref, gates_ref, part_ref, sgb_ref, sconv_ref):
    xb = x_ref[...].astype(BF16)
    za = _dot(xb, wa_ref[...])
    bg = za[:, 0:D_CONV]
    p = za[:, D_CONV:2 * D_CONV] * za[:, 2 * D_CONV:3 * D_CONV]
    cache0 = cache_ref[:, 0, :]
    cache1 = cache_ref[:, 1, :]
    wc = wconv_ref[...]
    conv = cache0 * wc[0:1, :] + cache1 * wc[1:2, :] + p * wc[2:3, :]
    sconv_ref[:, 0, :] = cache1
    sconv_ref[:, 1, :] = p
    ya = _dot((bg * conv).astype(BF16), wpa_ref[...])
    zg = _dot(xb, wg_ref[...])
    part_ref[...] = _sigmoid(zg[:, 0:D_MODEL]) * ya
    sgb_ref[...] = _sigmoid(zg[:, D_MODEL:2 * D_MODEL])
    gz = _dot(xb, wgc_ref[...]) + gbias_ref[...]
    lane = lax.broadcasted_iota(jnp.int32, gz.shape, 1)
    gates_ref[...] = jnp.where(lane < H_M, gz, _log_sigmoid(gz))
    q_ref[...] = _dot(xb, wqkvo_ref[:, 0:D_M])
    k_ref[...] = _dot(xb, wqkvo_ref[:, D_M:2 * D_M]) * K_SCALE
    v_ref[...] = _dot(xb, wqkvo_ref[:, 2 * D_M:3 * D_M])
    so_ref[...] = _sigmoid(_dot(xb, wqkvo_ref[:, 3 * D_M:4 * D_M]))


def _proj_sample(x, cache, layer, w):
    ns = x.shape[0]
    whole = lambda shape: pl.BlockSpec(shape, lambda i: (0,) * len(shape))
    act = jax.ShapeDtypeStruct((ns, D_MODEL), F32)
    in_specs = [
        whole((ns, D_MODEL)),
        pl.BlockSpec((None, ns, CONV_W - 1, D_CONV), lambda i: (layer, 0, 0, 0)),
        _resident(w["wa"].shape, layer),
        _resident(w["wqkvo"].shape, layer),
        _resident(w["wgc"].shape, layer),
        _resident(w["gbias_row"].shape, layer),
        _resident(w["wg"].shape, layer),
        _resident(w["wconv"].shape, layer),
        _resident(w["wpa"].shape, layer),
    ]
    out_shape = (act, act, act, act, jax.ShapeDtypeStruct((ns, LANES), F32), act, act,
                 jax.ShapeDtypeStruct((ns, CONV_W - 1, D_CONV), F32))
    out_specs = tuple(whole(s.shape) for s in out_shape)
    return pl.pallas_call(
        _proj_sample_kernel,
        grid=(1,),
        in_specs=in_specs,
        out_specs=out_specs,
        out_shape=out_shape,
        compiler_params=pltpu.CompilerParams(
            dimension_semantics=("arbitrary",), vmem_limit_bytes=VMEM_LIMIT_BYTES),
        name="proj_sample",
    )(x, cache, w["wa"], w["wqkvo"], w["wgc"], w["gbias_row"], w["wg"], w["wconv"], w["wpa"])


def _state_sample_kernel(q_ref, k_ref, v_ref, so_ref, gates_ref, m0_ref, n0_ref, c0_ref, mhg_ref,
                         *rest):
    hm_ref, c_ref, n_ref, m_ref, qc_ref = rest[-5:]
    nb = q_ref.shape[0]
    gates = gates_ref[...]
    zpad = jnp.zeros((LANES - nb, DH_M), F32)
    for h in range(H_M):
        cols = slice(h * DH_M, (h + 1) * DH_M)
        q = q_ref[:, cols]
        k = k_ref[:, cols]
        v = v_ref[:, cols]
        li = gates[:, h:h + 1]
        lf = gates[:, H_M + h:H_M + h + 1]
        m0 = m0_ref[:, h:h + 1]
        n0 = n0_ref[:, h, :]
        m_new = jnp.maximum(lf + m0, li)
        decay = jnp.exp(lf + m0 - m_new)
        wgt = jnp.exp(li - m_new)
        kw = k * wgt
        q_t = jnp.concatenate([q, zpad], axis=0).T
        kw_t = jnp.concatenate([kw, zpad], axis=0).T
        for i in range(nb):
            c0 = c0_ref[i, h]
            qc_ref[i:i + 1, :] = jnp.sum(q_t[:, i:i + 1] * c0, axis=0, keepdims=True)
            c_ref[i, h] = decay[i:i + 1, :] * c0 + kw_t[:, i:i + 1] * v[i:i + 1, :]
        s = jnp.sum(q * k, axis=-1, keepdims=True) * wgt
        numer = s * v + decay * qc_ref[...]
        den = s + decay * jnp.sum(q * n0, axis=-1, keepdims=True)
        hc = numer / jnp.maximum(jnp.abs(den), jnp.exp(-m_new))
        hm_ref[:, cols] = so_ref[:, cols] * (_head_norm(hc) * mhg_ref[:, cols])
        n_ref[:, h, :] = decay * n0 + kw
        m_ref[:, h:h + 1] = m_new


def _state_sample(q, k, v, so, gates, m0, n0, c0, layer, w, c_all):
    depth = c0.shape[0]
    ns = q.shape[0]
    nb = min(SEQ_BLOCK, ns)
    rows = lambda i: (i, 0)
    act_spec = pl.BlockSpec((nb, D_MODEL), rows)
    layer_slab = lambda i: (layer, i, 0, 0, 0)
    in_specs = [
        act_spec, act_spec, act_spec, act_spec,
        pl.BlockSpec((nb, LANES), rows),
        pl.BlockSpec((None, nb, H_M), lambda i: (layer, i, 0)),
        pl.BlockSpec((None, nb, H_M, DH_M), lambda i: (layer, i, 0, 0)),
        pl.BlockSpec((None, nb, H_M, DH_M, DH_M), layer_slab),
        pl.BlockSpec((None, 1, D_M), lambda i: (layer, 0, 0)),
    ]
    args = [q, k, v, so, gates, m0, n0, c0, w["mhg"]]
    aliases = {}
    if c_all is not None:
        in_specs.append(pl.BlockSpec(memory_space=pl.ANY))
        args.append(c_all)
        aliases = {len(args) - 1: 1}
    out_shape = (
        jax.ShapeDtypeStruct((ns, D_M), F32),
        jax.ShapeDtypeStruct((depth, ns, H_M, DH_M, DH_M), F32),
        jax.ShapeDtypeStruct((ns, H_M, DH_M), F32),
        jax.ShapeDtypeStruct((ns, H_M), F32),
    )
    out_specs = (
        act_spec,
        pl.BlockSpec((None, nb, H_M, DH_M, DH_M), layer_slab),
        pl.BlockSpec((nb, H_M, DH_M), lambda i: (i, 0, 0)),
        pl.BlockSpec((nb, H_M), rows),
    )
    return pl.pallas_call(
        _state_sample_kernel,
        grid=(ns // nb,),
        in_specs=in_specs,
        out_specs=out_specs,
        out_shape=out_shape,
        input_output_aliases=aliases,
        scratch_shapes=[pltpu.VMEM((nb, DH_M), F32)],
        compiler_params=pltpu.CompilerParams(
            dimension_semantics=("arbitrary",), vmem_limit_bytes=VMEM_LIMIT_BYTES),
        name="state_sample",
    )(*args)


def _out_sample_kernel(x_ref, hm_ref, part_ref, sgb_ref, fcache_ref, wpb_ref, wmix_ref, ln1g_ref,
                       ln1b_ref, wup_ref, wconv_ref, wdown_ref, ln2g_ref, ln2b_ref,
                       x2_ref, fnew_ref,
                       x1_ref, xb_ref, gate_ref, acc_ref):
    k = pl.program_id(0)

    @pl.when(k == 0)
    def _mix():
        yb = _dot(hm_ref[...].astype(BF16), wpb_ref[...])
        merged = part_ref[...] + sgb_ref[...] * yb
        y = ALPHA * x_ref[...] + _dot(merged.astype(BF16), wmix_ref[...])
        x1 = _layer_norm(y, ln1g_ref[...], ln1b_ref[...])
        x1_ref[...] = x1
        xb_ref[...] = x1.astype(BF16)
        acc_ref[...] = jnp.zeros(acc_ref.shape, F32)

    val = _dot(xb_ref[...], wup_ref[...])
    row1 = fcache_ref[:, 1, :]
    wc = wconv_ref[...]
    conv = fcache_ref[:, 0, :] * wc[0:1, :] + row1 * wc[1:2, :] + val * wc[2:3, :]
    fnew_ref[:, 0, :] = row1
    fnew_ref[:, 1, :] = val

    @pl.when(k < N_FF_CHUNKS)
    def _gate():
        gate_ref[k] = conv * _sigmoid(conv)

    @pl.when(k >= N_FF_CHUNKS)
    def _down():
        hdn = (gate_ref[k - N_FF_CHUNKS] * conv).astype(BF16)
        acc_ref[...] += _dot(hdn, wdown_ref[...])

    @pl.when(k == 2 * N_FF_CHUNKS - 1)
    def _finish():
        x2_ref[...] = _layer_norm(ALPHA * x1_ref[...] + acc_ref[...], ln2g_ref[...], ln2b_ref[...])


def _out_sample(x, hm, part, sgb, fcache, layer, w):
    ns = x.shape[0]
    act = pl.BlockSpec((ns, D_MODEL), lambda k: (0, 0))
    col_chunk = lambda k: (layer, 0, k)
    in_specs = [
        act, act, act, act,
        pl.BlockSpec((None, ns, CONV_W - 1, FF_CHUNK), lambda k: (layer, 0, 0, k)),
        _resident(w["wpb"].shape, layer),
        _resident(w["wmix"].shape, layer),
        _resident(w["ln1g"].shape, layer),
        _resident(w["ln1b"].shape, layer),
        pl.BlockSpec((None, D_MODEL, FF_CHUNK), col_chunk),
        pl.BlockSpec((None, CONV_W, FF_CHUNK), col_chunk),
        pl.BlockSpec((None, FF_CHUNK, D_MODEL),
                     lambda k: (layer, jnp.maximum(k - N_FF_CHUNKS, 0), 0)),
        _resident(w["ln2g"].shape, layer),
        _resident(w["ln2b"].shape, layer),
    ]
    names = ("wpb", "wmix", "ln1g", "ln1b", "wup", "wfconv", "wdown", "ln2g", "ln2b")
    return pl.pallas_call(
        _out_sample_kernel,
        grid=(2 * N_FF_CHUNKS,),
        in_specs=in_specs,
        out_specs=(act, pl.BlockSpec((ns, CONV_W - 1, FF_CHUNK), lambda k: (0, 0, k))),
        out_shape=(jax.ShapeDtypeStruct((ns, D_MODEL), F32),
                   jax.ShapeDtypeStruct((ns, CONV_W - 1, 2 * D_FF), F32)),
        scratch_shapes=[
            pltpu.VMEM((ns, D_MODEL), F32),
            pltpu.VMEM((ns, D_MODEL), BF16),
            pltpu.VMEM((N_FF_CHUNKS, ns, FF_CHUNK), F32),
            pltpu.VMEM((ns, D_MODEL), F32),
        ],
        compiler_params=pltpu.CompilerParams(
            dimension_semantics=("arbitrary",), vmem_limit_bytes=VMEM_LIMIT_BYTES),
        name="out_sample",
    )(x, hm, part, sgb, fcache, *[w[n] for n in names])


def _transpose_cast_kernel(w_ref, o_ref):
    o_ref[...] = w_ref[...].T.astype(BF16)


def _k_weights_transposed(w_in, k_offset):
    depth = w_in.shape[0]
    first_block = k_offset // DH_M
    assert first_block * DH_M == k_offset
    return pl.pallas_call(
        _transpose_cast_kernel,
        grid=(depth, H_M),
        in_specs=[pl.BlockSpec((None, D_MODEL, DH_M), lambda l, h: (l, 0, first_block + h))],
        out_specs=pl.BlockSpec((None, DH_M, D_MODEL), lambda l, h: (l, h, 0)),
        out_shape=jax.ShapeDtypeStruct((depth, D_M, D_MODEL), BF16),
        compiler_params=pltpu.CompilerParams(dimension_semantics=("arbitrary", "arbitrary")),
        name="k_weights_transposed",
    )(w_in)


def _pack_weights(w_in, b_igate, b_fgate, w_conv_mix, mhln_g, w_proj_a, w_proj_b, w_mix_out,
                  ln1_g, ln1_b, w_ffn_up, w_ffn_conv, w_ffn_down, ln2_g, ln2_b):
    depth = w_in.shape[0]
    o_a = 3 * D_CONV
    o_gate = o_a + 4 * D_M
    o_g = o_gate + 2 * H_M
    wgate = w_in[:, :, o_gate:o_g]
    gbias = jnp.concatenate([b_igate, b_fgate], axis=-1)
    wgc = jnp.pad(wgate, ((0, 0), (0, 0), (0, LANES - 2 * H_M)))
    gbias_row = jnp.pad(gbias, ((0, 0), (0, LANES - 2 * H_M)))[:, None, :]

    row = lambda a: a[:, None, :]
    return {
        "wa": w_in[:, :, 0:o_a].astype(BF16),
        "wqkvo": w_in[:, :, o_a:o_gate].astype(BF16),
        "wkt": _k_weights_transposed(w_in, o_a + D_M),
        "wgt": wgate.transpose(0, 2, 1).astype(BF16),
        "gbias": gbias[:, :, None],
        "wgc": wgc.astype(BF16),
        "gbias_row": gbias_row,
        "wg": w_in[:, :, o_g:o_g + 2 * D_MODEL].astype(BF16),
        "wconv": w_conv_mix,
        "mhg": row(mhln_g),
        "wpa": w_proj_a.astype(BF16),
        "wpb": w_proj_b.astype(BF16),
        "wmix": w_mix_out.astype(BF16),
        "ln1g": row(ln1_g), "ln1b": row(ln1_b),
        "wup": w_ffn_up.astype(BF16),
        "wfconv": w_ffn_conv,
        "wdown": w_ffn_down.astype(BF16),
        "ln2g": row(ln2_g), "ln2b": row(ln2_b),
    }


def kernel(x_prompt, x_sample, cache_sconv, state_mlstm_C, state_mlstm_n, state_mlstm_m, cache_ffn_conv, w_in, b_igate, b_fgate, w_conv_mix, mhln_g, w_proj_a, w_proj_b, w_mix_out, ln1_g, ln1_b, w_ffn_up, w_ffn_conv, w_ffn_down, ln2_g, ln2_b):
    depth = w_in.shape[0]
    ns = x_sample.shape[0]
    assert x_sample.shape[1] == 1
    assert x_prompt.shape[1] % CHUNK == 0
    w = _pack_weights(w_in, b_igate, b_fgate, w_conv_mix, mhln_g, w_proj_a, w_proj_b, w_mix_out,
                      ln1_g, ln1_b, w_ffn_up, w_ffn_conv, w_ffn_down, ln2_g, ln2_b)

    xp = x_prompt
    xs = x_sample.reshape(ns, D_MODEL)
    outs = [[] for _ in range(9)]
    c_sample = None
    for l in range(depth):
        xp, sp, cp, np_, mp = _mixer_prompt(xp, l, w)
        xp, fp = _ffn_prompt(xp, l, w)

        q, k, v, so, gates, part, sgb, ss = _proj_sample(xs, cache_sconv, l, w)
        hm, c_sample, ns_, ms = _state_sample(q, k, v, so, gates, state_mlstm_m, state_mlstm_n,
                                              state_mlstm_C, l, w, c_sample)
        xs, fs = _out_sample(xs, hm, part, sgb, cache_ffn_conv, l, w)

        for lst, val in zip(outs, (sp, ss, cp, np_, ns_, mp, ms, fp, fs)):
            lst.append(val)

    sp, ss, cp, np_, ns_, mp, ms, fp, fs = (jnp.stack(o) for o in outs)
    return (xp, xs.reshape(ns, 1, D_MODEL), sp, ss, cp, c_sample, np_, ns_, mp, ms, fp, fs)
```

```python
import functools

import jax
import jax.numpy as jnp
from jax import lax
from jax.experimental import pallas as pl
from jax.experimental.pallas import tpu as pltpu

F32 = jnp.float32
BF16 = jnp.bfloat16

D_MODEL = 1024
D_CONV = 512
H_M = 4
DH_M = 256
D_M = H_M * DH_M
CONV_W = 3
D_FF = 2816
EPS = 1e-5
DEPTH = 4
ALPHA = (2.0 * DEPTH) ** 0.25
K_SCALE = DH_M ** -0.5

SUBLANES = 8
LANES = 128
VMEM_LIMIT_BYTES = 56 * 1024 * 1024

CHUNK = 128
MIX_CHUNK = 256
TILE_T = 512
FF_CHUNK = 256
N_FF_CHUNKS = D_FF // FF_CHUNK
FFN_ROW_BLOCKS = 2
SEQ_BLOCK = 8
HALO_ROW = SUBLANES - (CONV_W - 1)


def _dot(a, b):
    return jnp.dot(a, b, preferred_element_type=F32)


def _dot_nt(a, b):
    return lax.dot_general(a, b, (((1,), (1,)), ((), ())), preferred_element_type=F32)


def _sigmoid(x):
    return 1.0 / (1.0 + jnp.exp(-x))


def _log_sigmoid(x):
    return -(jnp.maximum(-x, 0.0) + jnp.log1p(jnp.exp(-jnp.abs(x))))


def _layer_norm(y, g, b):
    mu = jnp.mean(y, axis=-1, keepdims=True)
    yc = y - mu
    var = jnp.mean(yc * yc, axis=-1, keepdims=True)
    return yc * lax.rsqrt(var + EPS) * g + b


def _head_norm(h):
    mu = jnp.mean(h, axis=-1, keepdims=True)
    hc = h - mu
    var = jnp.mean(hc * hc, axis=-1, keepdims=True)
    return hc * lax.rsqrt(var + EPS)


def _resident(shape, layer):
    nd = len(shape)
    return pl.BlockSpec((None,) + tuple(shape[1:]),
                        lambda *_: (layer,) + (0,) * (nd - 1),
                        pipeline_mode=pl.Buffered(1))


def _lane_scan(x, op, fill, lane_id):
    shift = 1
    while shift < x.shape[1]:
        x = op(x, jnp.where(lane_id >= shift, pltpu.roll(x, shift, 1), fill))
        shift *= 2
    return x


def _mixer_prompt_kernel(x_ref, wa_ref, wqkvo_ref, wkt_ref, wgt_ref, gbias_ref, wg_ref, wconv_ref,
                         mhg_ref, wpa_ref, wpb_ref, wmix_ref, lng_ref, lnb_ref,
                         x1_ref, sconv_ref, c_ref, n_ref, m_ref,
                         xb_ref, pbuf_ref, merged_ref, q_ref, kt_ref, va_ref, o_ref, hm_ref,
                         caug_ref, mlane_ref, r_ref, w_ref, scol_ref, colmx_ref, colsc_ref,
                         coleinv_ref):
    t = pl.program_id(1)
    tile = x_ref.shape[0]
    L = MIX_CHUNK
    n_chunks = tile // L

    @pl.when(t == 0)
    def _init():
        pbuf_ref[0:SUBLANES, :] = jnp.zeros((SUBLANES, D_CONV), F32)
        caug_ref[...] = jnp.zeros(caug_ref.shape, F32)
        m_ref[...] = jnp.zeros(m_ref.shape, F32)
        mlane_ref[...] = jnp.zeros(mlane_ref.shape, F32)

    x = x_ref[...]
    xb_ref[...] = x.astype(BF16)
    xb = xb_ref[...]

    za = _dot(xb, wa_ref[...])
    bg = za[:, 0:D_CONV]
    p = za[:, D_CONV:2 * D_CONV] * za[:, 2 * D_CONV:3 * D_CONV]
    pbuf_ref[SUBLANES:SUBLANES + tile, :] = p
    wc = wconv_ref[...]
    conv = (pbuf_ref[HALO_ROW:HALO_ROW + tile, :] * wc[0:1, :]
            + pbuf_ref[HALO_ROW + 1:HALO_ROW + 1 + tile, :] * wc[1:2, :]
            + p * wc[2:3, :])
    halo = pbuf_ref[tile:tile + SUBLANES, :]
    pbuf_ref[0:SUBLANES, :] = halo
    sconv_ref[...] = halo
    ya = _dot((bg * conv).astype(BF16), wpa_ref[...])
    ga = _dot(xb, wg_ref[:, 0:D_MODEL])
    merged_ref[...] = _sigmoid(ga) * ya

    gates = _dot_nt(wgt_ref[...], xb) + gbias_ref[...]
    va_ref[:, :, DH_M:DH_M + LANES] = jnp.ones((H_M, tile, LANES), BF16)
    for h in range(H_M):
        hc_ = slice(h * DH_M, (h + 1) * DH_M)
        q_ref[:, hc_] = _dot(xb_ref[...], wqkvo_ref[:, hc_]).astype(BF16)
        kt_ref[h] = _dot_nt(wkt_ref[hc_, :], xb_ref[...]) * K_SCALE
        va_ref[h, :, 0:DH_M] = _dot(
            xb_ref[...], wqkvo_ref[:, 2 * D_M + h * DH_M:2 * D_M + (h + 1) * DH_M]).astype(BF16)
        o_ref[:, hc_] = _dot(
            xb_ref[...], wqkvo_ref[:, 3 * D_M + h * DH_M:3 * D_M + (h + 1) * DH_M])

    row_id = lax.broadcasted_iota(jnp.int32, (2 * H_M, L), 0)
    lane_id = lax.broadcasted_iota(jnp.int32, (2 * H_M, L), 1)
    head_rows = lax.broadcasted_iota(jnp.int32, (2 * H_M, 1), 0) < H_M
    m8 = m_ref[:, 0:1]
    mlane = mlane_ref[0:1, :]
    decays = []
    for c in range(n_chunks):
        g = gates[:, c * L:(c + 1) * L]
        csum = _lane_scan(_log_sigmoid(g), jnp.add, 0.0, lane_id)
        b4 = pltpu.roll(csum, H_M, 0)
        r = g - b4
        gmax = _lane_scan(r, jnp.maximum, -jnp.inf, lane_id)
        comb = jnp.where(row_id < H_M, gmax, csum)
        scol = jnp.concatenate([comb, jnp.zeros((LANES - 2 * H_M, L), F32)], axis=0).T
        scol_ref[c * L:(c + 1) * L, :] = scol

        mx_last = jnp.maximum(m8, gmax[:, L - 1:L])
        decays.append(jnp.exp(m8 - mx_last))
        r_ref[c] = r
        w_ref[c] = jnp.exp(r - mx_last)
        m8 = jnp.where(head_rows, b4[:, L - 1:L] + mx_last, 0.0)

        sc = scol_ref[c * L:(c + 1) * L, :]
        mxc = jnp.maximum(mlane, sc)
        colmx_ref[c * L:(c + 1) * L, :] = mxc
        colsc_ref[c * L:(c + 1) * L, :] = jnp.exp(mlane - mxc)
        coleinv_ref[c * L:(c + 1) * L, :] = jnp.exp(-(pltpu.roll(sc, LANES - H_M, 1) + mxc))
        last = sc[L - 1:L, :]
        mlane = pltpu.roll(last, LANES - H_M, 1) + jnp.maximum(mlane, last)
    m_ref[...] = jnp.broadcast_to(m8, m_ref.shape)
    mlane_ref[...] = jnp.broadcast_to(mlane, mlane_ref.shape)

    tri = (lax.broadcasted_iota(jnp.int32, (L, L), 0)
           >= lax.broadcasted_iota(jnp.int32, (L, L), 1))

    for c in range(n_chunks):
        rows = slice(c * L, (c + 1) * L)
        for h in range(H_M):
            hc_ = slice(h * DH_M, (h + 1) * DH_M)
            q = q_ref[rows, hc_]
            kt = kt_ref[h, :, rows]
            va = va_ref[h, rows, :]
            r_row = r_ref[c, h:h + 1, :]
            w_row = w_ref[c, h:h + 1, :]
            mx_col = colmx_ref[rows, h:h + 1]
            sc_col = colsc_ref[rows, h:h + 1]
            einv_col = coleinv_ref[rows, h:h + 1]
            decay = decays[c][h:h + 1, :]

            pmat = jnp.where(tri, jnp.exp(r_row - mx_col), 0.0)
            s = (_dot(q, kt.astype(BF16)) * pmat).astype(BF16)
            ca = caug_ref[h]
            tot = _dot(s, va) + sc_col * _dot(q, ca.astype(BF16))
            dd = jnp.maximum(jnp.abs(tot[:, DH_M:DH_M + LANES]), einv_col)
            hcell = tot[:, 0:DH_M] / jnp.concatenate([dd, dd], axis=1)
            hm = _sigmoid(o_ref[rows, hc_]) * (_head_norm(hcell) * mhg_ref[:, hc_])
            hm_ref[rows, hc_] = hm.astype(BF16)

            kw = (kt * w_row).astype(BF16)
            caug_ref[h] = decay * ca + _dot(kw, va)

    @pl.when(t == pl.num_programs(1) - 1)
    def _emit_state():
        c_ref[...] = caug_ref[:, :, 0:DH_M]
        n_ref[...] = caug_ref[:, :, DH_M:DH_M + LANES]

    yb = _dot(hm_ref[...], wpb_ref[...])
    gb = _dot(xb_ref[...], wg_ref[:, D_MODEL:2 * D_MODEL])
    merged = merged_ref[...] + _sigmoid(gb) * yb
    y = ALPHA * x_ref[...] + _dot(merged.astype(BF16), wmix_ref[...])
    x1_ref[...] = _layer_norm(y, lng_ref[...], lnb_ref[...])


def _mixer_prompt(x, layer, w):
    bsz, seq, _ = x.shape
    tile = min(TILE_T, seq)
    n_tiles = seq // tile
    n_chunks = tile // MIX_CHUNK
    row_block = lambda b, t: (b, t, 0)
    per_seq = lambda b, t: (b, 0, 0)
    per_seq4 = lambda b, t: (b, 0, 0, 0)
    names = ("wa", "wqkvo", "wkt", "wgt", "gbias", "wg", "wconv", "mhg", "wpa", "wpb", "wmix",
             "ln1g", "ln1b")
    in_specs = [pl.BlockSpec((None, tile, D_MODEL), row_block)]
    in_specs += [_resident(w[n].shape, layer) for n in names]
    out_shape = (
        jax.ShapeDtypeStruct((bsz, seq, D_MODEL), F32),
        jax.ShapeDtypeStruct((bsz, SUBLANES, D_CONV), F32),
        jax.ShapeDtypeStruct((bsz, H_M, DH_M, DH_M), F32),
        jax.ShapeDtypeStruct((bsz, H_M, DH_M, LANES), F32),
        jax.ShapeDtypeStruct((bsz, SUBLANES, LANES), F32),
    )
    out_specs = (
        pl.BlockSpec((None, tile, D_MODEL), row_block),
        pl.BlockSpec((None, SUBLANES, D_CONV), per_seq),
        pl.BlockSpec((None, H_M, DH_M, DH_M), per_seq4),
        pl.BlockSpec((None, H_M, DH_M, LANES), per_seq4),
        pl.BlockSpec((None, SUBLANES, LANES), per_seq),
    )
    col = pltpu.VMEM((tile, LANES), F32)
    scratch = [
        pltpu.VMEM((tile, D_MODEL), BF16),
        pltpu.VMEM((tile + SUBLANES, D_CONV), F32),
        pltpu.VMEM((tile, D_MODEL), F32),
        pltpu.VMEM((tile, D_M), BF16),
        pltpu.VMEM((H_M, DH_M, tile), F32),
        pltpu.VMEM((H_M, tile, DH_M + LANES), BF16),
        pltpu.VMEM((tile, D_M), F32),
        pltpu.VMEM((tile, D_M), BF16),
        pltpu.VMEM((H_M, DH_M, DH_M + LANES), F32),
        pltpu.VMEM((SUBLANES, LANES), F32),
        pltpu.VMEM((n_chunks, 2 * H_M, MIX_CHUNK), F32),
        pltpu.VMEM((n_chunks, 2 * H_M, MIX_CHUNK), F32),
        col, col, col, col,
    ]
    x1, sconv, c, n, m = pl.pallas_call(
        _mixer_prompt_kernel,
        grid=(bsz, n_tiles),
        in_specs=in_specs,
        out_specs=out_specs,
        out_shape=out_shape,
        scratch_shapes=scratch,
        compiler_params=pltpu.CompilerParams(
            dimension_semantics=("arbitrary", "arbitrary"),
            vmem_limit_bytes=VMEM_LIMIT_BYTES),
        name="mixer_prompt",
    )(x, *[w[n] for n in names])
    return x1, sconv[:, HALO_ROW:, :], c, n[:, :, :, 0], m[:, :H_M, 0]


def _ffn_prompt_kernel(x_ref, wup_ref, wconv_ref, wdown_ref, lng_ref, lnb_ref,
                       x2_ref, halo_ref,
                       xb_ref, acc_ref):
    t = pl.program_id(1)
    tile = x_ref.shape[0]

    @pl.when(t == 0)
    def _init():
        halo_ref[...] = jnp.zeros(halo_ref.shape, F32)

    xb_ref[...] = x_ref[...].astype(BF16)

    rb = tile // FFN_ROW_BLOCKS
    sub = lax.broadcasted_iota(jnp.int32, (SUBLANES, FF_CHUNK), 0)

    def shifted(val, prev8, k):
        rolled = pltpu.roll(val, k, 0)
        top = jnp.where(sub < k, pltpu.roll(prev8, k, 0), rolled[0:SUBLANES, :])
        return jnp.concatenate([top, rolled[SUBLANES:, :]], axis=0)

    def conv(val, prev8, cols):
        wc = wconv_ref[:, cols]
        return (shifted(val, prev8, 2) * wc[0:1, :] + shifted(val, prev8, 1) * wc[1:2, :]
                + val * wc[2:3, :])

    chunk_cols = lambda j: (slice(j * FF_CHUNK, (j + 1) * FF_CHUNK),
                            slice(D_FF + j * FF_CHUNK, D_FF + (j + 1) * FF_CHUNK))

    def up_project(j, r):
        xr = xb_ref[r * rb:(r + 1) * rb, :]
        return tuple(_dot(xr, wup_ref[:, cols]) for cols in chunk_cols(j))

    pending = {(0, r): up_project(0, r) for r in range(FFN_ROW_BLOCKS)}
    for j in range(N_FF_CHUNKS):
        prev = [halo_ref[:, cols] for cols in chunk_cols(j)]
        for r in range(FFN_ROW_BLOCKS):
            rows = slice(r * rb, (r + 1) * rb)
            vals = pending.pop((j, r))
            g, u = (conv(v, p8, cols) for v, p8, cols in zip(vals, prev, chunk_cols(j)))
            prev = [v[rb - SUBLANES:rb, :] for v in vals]
            hdn = (g * _sigmoid(g) * u).astype(BF16)
            if j + 1 < N_FF_CHUNKS:
                pending[(j + 1, r)] = up_project(j + 1, r)
            part = _dot(hdn, wdown_ref[j * FF_CHUNK:(j + 1) * FF_CHUNK, :])
            if j == 0:
                acc_ref[rows, :] = part
            else:
                acc_ref[rows, :] += part
        for cols, tail in zip(chunk_cols(j), prev):
            halo_ref[:, cols] = tail

    y = ALPHA * x_ref[...] + acc_ref[...]
    x2_ref[...] = _layer_norm(y, lng_ref[...], lnb_ref[...])


def _ffn_prompt(x, layer, w):
    bsz, seq, _ = x.shape
    tile = min(TILE_T, seq)
    n_tiles = seq // tile
    row_block = lambda b, t: (b, t, 0)
    in_specs = [
        pl.BlockSpec((None, tile, D_MODEL), row_block),
        _resident(w["wup"].shape, layer),
        _resident(w["wfconv"].shape, layer),
        _resident(w["wdown"].shape, layer),
        _resident(w["ln2g"].shape, layer),
        _resident(w["ln2b"].shape, layer),
    ]
    x2, halo = pl.pallas_call(
        _ffn_prompt_kernel,
        grid=(bsz, n_tiles),
        in_specs=in_specs,
        out_specs=(pl.BlockSpec((None, tile, D_MODEL), row_block),
                   pl.BlockSpec((None, SUBLANES, 2 * D_FF), lambda b, t: (b, 0, 0))),
        out_shape=(jax.ShapeDtypeStruct((bsz, seq, D_MODEL), F32),
                   jax.ShapeDtypeStruct((bsz, SUBLANES, 2 * D_FF), F32)),
        scratch_shapes=[
            pltpu.VMEM((tile, D_MODEL), BF16),
            pltpu.VMEM((tile, D_MODEL), F32),
        ],
        compiler_params=pltpu.CompilerParams(
            dimension_semantics=("arbitrary", "arbitrary"),
            vmem_limit_bytes=VMEM_LIMIT_BYTES),
        name="ffn_prompt",
    )(x, w["wup"], w["wfconv"], w["wdown"], w["ln2g"], w["ln2b"])
    return x2, halo[:, HALO_ROW:, :]


def _proj_sample_kernel(x_ref, older_ref, newer_ref, wa_ref, wqkvo_ref, wgc_ref, gbias_ref, wg_ref,
                        wconv_ref, wpa_ref,
                        q_ref, k_ref, v_ref, so_ref, gates_ref, part_ref, sgb_ref, p_ref):
    xb = x_ref[...].astype(BF16)
    za = _dot(xb, wa_ref[...])
    bg = za[:, 0:D_CONV]
    p = za[:, D_CONV:2 * D_CONV] * za[:, 2 * D_CONV:3 * D_CONV]
    wc = wconv_ref[...]
    conv = older_ref[...] * wc[0:1, :] + newer_ref[...] * wc[1:2, :] + p * wc[2:3, :]
    p_ref[...] = p
    ya = _dot((bg * conv).astype(BF16), wpa_ref[...])
    zg = _dot(xb, wg_ref[...])
    part_ref[...] = _sigmoid(zg[:, 0:D_MODEL]) * ya
    sgb_ref[...] = _sigmoid(zg[:, D_MODEL:2 * D_MODEL])
    gz = _dot(xb, wgc_ref[...]) + gbias_ref[...]
    lane = lax.broadcasted_iota(jnp.int32, gz.shape, 1)
    gates_ref[...] = jnp.where(lane < H_M, gz, _log_sigmoid(gz))
    q_ref[...] = _dot(xb, wqkvo_ref[:, 0:D_M])
    k_ref[...] = _dot(xb, wqkvo_ref[:, D_M:2 * D_M]) * K_SCALE
    v_ref[...] = _dot(xb, wqkvo_ref[:, 2 * D_M:3 * D_M])
    so_ref[...] = _sigmoid(_dot(xb, wqkvo_ref[:, 3 * D_M:4 * D_M]))


def _proj_sample(x, older, newer, layer, w):
    ns = x.shape[0]
    whole = lambda shape: pl.BlockSpec(shape, lambda i: (0,) * len(shape))
    act = jax.ShapeDtypeStruct((ns, D_MODEL), F32)
    in_specs = [
        whole((ns, D_MODEL)),
        pl.BlockSpec((None, ns, D_CONV), lambda i: (layer, 0, 0)),
        pl.BlockSpec((None, ns, D_CONV), lambda i: (layer, 0, 0)),
        _resident(w["wa"].shape, layer),
        _resident(w["wqkvo"].shape, layer),
        _resident(w["wgc"].shape, layer),
        _resident(w["gbias_row"].shape, layer),
        _resident(w["wg"].shape, layer),
        _resident(w["wconv"].shape, layer),
        _resident(w["wpa"].shape, layer),
    ]
    out_shape = (act, act, act, act, jax.ShapeDtypeStruct((ns, LANES), F32), act, act,
                 jax.ShapeDtypeStruct((ns, D_CONV), F32))
    out_specs = tuple(whole(s.shape) for s in out_shape)
    return pl.pallas_call(
        _proj_sample_kernel,
        grid=(1,),
        in_specs=in_specs,
        out_specs=out_specs,
        out_shape=out_shape,
        compiler_params=pltpu.CompilerParams(
            dimension_semantics=("arbitrary",), vmem_limit_bytes=VMEM_LIMIT_BYTES),
        name="proj_sample",
    )(x, older, newer, w["wa"], w["wqkvo"], w["wgc"], w["gbias_row"], w["wg"], w["wconv"], w["wpa"])


def _state_sample_kernel(q_ref, k_ref, v_ref, so_ref, gates_ref, m0_ref, n0_ref, c0_ref, mhg_ref,
                         *rest):
    hm_ref, c_ref, n_ref, m_ref, qc_ref = rest[-5:]
    nb = q_ref.shape[0]
    gates = gates_ref[...]
    zpad = jnp.zeros((LANES - nb, DH_M), F32)
    for h in range(H_M):
        cols = slice(h * DH_M, (h + 1) * DH_M)
        q = q_ref[:, cols]
        k = k_ref[:, cols]
        v = v_ref[:, cols]
        li = gates[:, h:h + 1]
        lf = gates[:, H_M + h:H_M + h + 1]
        m0 = m0_ref[:, h:h + 1]
        n0 = n0_ref[:, h, :]
        m_new = jnp.maximum(lf + m0, li)
        decay = jnp.exp(lf + m0 - m_new)
        wgt = jnp.exp(li - m_new)
        kw = k * wgt
        q_t = jnp.concatenate([q, zpad], axis=0).T
        kw_t = jnp.concatenate([kw, zpad], axis=0).T
        for i in range(nb):
            c0 = c0_ref[i, h]
            qc_ref[i:i + 1, :] = jnp.sum(q_t[:, i:i + 1] * c0, axis=0, keepdims=True)
            c_ref[i, h] = decay[i:i + 1, :] * c0 + kw_t[:, i:i + 1] * v[i:i + 1, :]
        s = jnp.sum(q * k, axis=-1, keepdims=True) * wgt
        numer = s * v + decay * qc_ref[...]
        den = s + decay * jnp.sum(q * n0, axis=-1, keepdims=True)
        hc = numer / jnp.maximum(jnp.abs(den), jnp.exp(-m_new))
        hm_ref[:, cols] = so_ref[:, cols] * (_head_norm(hc) * mhg_ref[:, cols])
        n_ref[:, h, :] = decay * n0 + kw
        m_ref[:, h:h + 1] = m_new


def _state_sample(q, k, v, so, gates, m0, n0, c0, layer, w, c_all):
    depth = c0.shape[0]
    ns = q.shape[0]
    nb = min(SEQ_BLOCK, ns)
    rows = lambda i: (i, 0)
    act_spec = pl.BlockSpec((nb, D_MODEL), rows)
    layer_slab = lambda i: (layer, i, 0, 0, 0)
    in_specs = [
        act_spec, act_spec, act_spec, act_spec,
        pl.BlockSpec((nb, LANES), rows),
        pl.BlockSpec((None, nb, H_M), lambda i: (layer, i, 0)),
        pl.BlockSpec((None, nb, H_M, DH_M), lambda i: (layer, i, 0, 0)),
        pl.BlockSpec((None, nb, H_M, DH_M, DH_M), layer_slab),
        pl.BlockSpec((None, 1, D_M), lambda i: (layer, 0, 0)),
    ]
    args = [q, k, v, so, gates, m0, n0, c0, w["mhg"]]
    aliases = {}
    if c_all is not None:
        in_specs.append(pl.BlockSpec(memory_space=pl.ANY))
        args.append(c_all)
        aliases = {len(args) - 1: 1}
    out_shape = (
        jax.ShapeDtypeStruct((ns, D_M), F32),
        jax.ShapeDtypeStruct((depth, ns, H_M, DH_M, DH_M), F32),
        jax.ShapeDtypeStruct((ns, H_M, DH_M), F32),
        jax.ShapeDtypeStruct((ns, H_M), F32),
    )
    out_specs = (
        act_spec,
        pl.BlockSpec((None, nb, H_M, DH_M, DH_M), layer_slab),
        pl.BlockSpec((nb, H_M, DH_M), lambda i: (i, 0, 0)),
        pl.BlockSpec((nb, H_M), rows),
    )
    return pl.pallas_call(
        _state_sample_kernel,
        grid=(ns // nb,),
        in_specs=in_specs,
        out_specs=out_specs,
        out_shape=out_shape,
        input_output_aliases=aliases,
        scratch_shapes=[pltpu.VMEM((nb, DH_M), F32)],
        compiler_params=pltpu.CompilerParams(
            dimension_semantics=("arbitrary",), vmem_limit_bytes=VMEM_LIMIT_BYTES),
        name="state_sample",
    )(*args)


def _out_sample_kernel(x_ref, hm_ref, part_ref, sgb_ref, older_ref, newer_ref, wpb_ref, wmix_ref,
                       ln1g_ref, ln1b_ref, wup_ref, wconv_ref, wdown_ref, ln2g_ref, ln2b_ref,
                       x2_ref, up_ref,
                       x1_ref, xb_ref, gate_ref, acc_ref):
    k = pl.program_id(0)

    @pl.when(k == 0)
    def _mix():
        yb = _dot(hm_ref[...].astype(BF16), wpb_ref[...])
        merged = part_ref[...] + sgb_ref[...] * yb
        y = ALPHA * x_ref[...] + _dot(merged.astype(BF16), wmix_ref[...])
        x1 = _layer_norm(y, ln1g_ref[...], ln1b_ref[...])
        x1_ref[...] = x1
        xb_ref[...] = x1.astype(BF16)
        acc_ref[...] = jnp.zeros(acc_ref.shape, F32)

    val = _dot(xb_ref[...], wup_ref[...])
    wc = wconv_ref[...]
    conv = older_ref[...] * wc[0:1, :] + newer_ref[...] * wc[1:2, :] + val * wc[2:3, :]
    up_ref[...] = val

    @pl.when(k < N_FF_CHUNKS)
    def _gate():
        gate_ref[k] = conv * _sigmoid(conv)

    @pl.when(k >= N_FF_CHUNKS)
    def _down():
        hdn = (gate_ref[k - N_FF_CHUNKS] * conv).astype(BF16)
        acc_ref[...] += _dot(hdn, wdown_ref[...])

    @pl.when(k == 2 * N_FF_CHUNKS - 1)
    def _finish():
        x2_ref[...] = _layer_norm(ALPHA * x1_ref[...] + acc_ref[...], ln2g_ref[...], ln2b_ref[...])


def _out_sample(x, hm, part, sgb, older, newer, layer, w):
    ns = x.shape[0]
    act = pl.BlockSpec((ns, D_MODEL), lambda k: (0, 0))
    col_chunk = lambda k: (layer, 0, k)
    in_specs = [
        act, act, act, act,
        pl.BlockSpec((None, ns, FF_CHUNK), col_chunk),
        pl.BlockSpec((None, ns, FF_CHUNK), col_chunk),
        _resident(w["wpb"].shape, layer),
        _resident(w["wmix"].shape, layer),
        _resident(w["ln1g"].shape, layer),
        _resident(w["ln1b"].shape, layer),
        pl.BlockSpec((None, D_MODEL, FF_CHUNK), col_chunk),
        pl.BlockSpec((None, CONV_W, FF_CHUNK), col_chunk),
        pl.BlockSpec((None, FF_CHUNK, D_MODEL),
                     lambda k: (layer, jnp.maximum(k - N_FF_CHUNKS, 0), 0)),
        _resident(w["ln2g"].shape, layer),
        _resident(w["ln2b"].shape, layer),
    ]
    names = ("wpb", "wmix", "ln1g", "ln1b", "wup", "wfconv", "wdown", "ln2g", "ln2b")
    return pl.pallas_call(
        _out_sample_kernel,
        grid=(2 * N_FF_CHUNKS,),
        in_specs=in_specs,
        out_specs=(act, pl.BlockSpec((ns, FF_CHUNK), lambda k: (0, k))),
        out_shape=(jax.ShapeDtypeStruct((ns, D_MODEL), F32),
                   jax.ShapeDtypeStruct((ns, 2 * D_FF), F32)),
        scratch_shapes=[
            pltpu.VMEM((ns, D_MODEL), F32),
            pltpu.VMEM((ns, D_MODEL), BF16),
            pltpu.VMEM((N_FF_CHUNKS, ns, FF_CHUNK), F32),
            pltpu.VMEM((ns, D_MODEL), F32),
        ],
        compiler_params=pltpu.CompilerParams(
            dimension_semantics=("arbitrary",), vmem_limit_bytes=VMEM_LIMIT_BYTES),
        name="out_sample",
    )(x, hm, part, sgb, older, newer, *[w[n] for n in names])


def _transpose_cast_kernel(w_ref, o_ref):
    o_ref[...] = w_ref[...].T.astype(BF16)


def _k_weights_transposed(w_in, k_offset):
    depth = w_in.shape[0]
    first_block = k_offset // DH_M
    assert first_block * DH_M == k_offset
    return pl.pallas_call(
        _transpose_cast_kernel,
        grid=(depth, H_M),
        in_specs=[pl.BlockSpec((None, D_MODEL, DH_M), lambda l, h: (l, 0, first_block + h))],
        out_specs=pl.BlockSpec((None, DH_M, D_MODEL), lambda l, h: (l, h, 0)),
        out_shape=jax.ShapeDtypeStruct((depth, D_M, D_MODEL), BF16),
        compiler_params=pltpu.CompilerParams(dimension_semantics=("arbitrary", "arbitrary")),
        name="k_weights_transposed",
    )(w_in)


def _pack_weights(w_in, b_igate, b_fgate, w_conv_mix, mhln_g, w_proj_a, w_proj_b, w_mix_out,
                  ln1_g, ln1_b, w_ffn_up, w_ffn_conv, w_ffn_down, ln2_g, ln2_b):
    depth = w_in.shape[0]
    o_a = 3 * D_CONV
    o_gate = o_a + 4 * D_M
    o_g = o_gate + 2 * H_M
    wgate = w_in[:, :, o_gate:o_g]
    gbias = jnp.concatenate([b_igate, b_fgate], axis=-1)
    wgc = jnp.pad(wgate, ((0, 0), (0, 0), (0, LANES - 2 * H_M)))
    gbias_row = jnp.pad(gbias, ((0, 0), (0, LANES - 2 * H_M)))[:, None, :]

    row = lambda a: a[:, None, :]
    return {
        "wa": w_in[:, :, 0:o_a].astype(BF16),
        "wqkvo": w_in[:, :, o_a:o_gate].astype(BF16),
        "wkt": _k_weights_transposed(w_in, o_a + D_M),
        "wgt": wgate.transpose(0, 2, 1).astype(BF16),
        "gbias": gbias[:, :, None],
        "wgc": wgc.astype(BF16),
        "gbias_row": gbias_row,
        "wg": w_in[:, :, o_g:o_g + 2 * D_MODEL].astype(BF16),
        "wconv": w_conv_mix,
        "mhg": row(mhln_g),
        "wpa": w_proj_a.astype(BF16),
        "wpb": w_proj_b.astype(BF16),
        "wmix": w_mix_out.astype(BF16),
        "ln1g": row(ln1_g), "ln1b": row(ln1_b),
        "wup": w_ffn_up.astype(BF16),
        "wfconv": w_ffn_conv,
        "wdown": w_ffn_down.astype(BF16),
        "ln2g": row(ln2_g), "ln2b": row(ln2_b),
    }


def kernel(x_prompt, x_sample, cache_sconv, state_mlstm_C, state_mlstm_n, state_mlstm_m, cache_ffn_conv, w_in, b_igate, b_fgate, w_conv_mix, mhln_g, w_proj_a, w_proj_b, w_mix_out, ln1_g, ln1_b, w_ffn_up, w_ffn_conv, w_ffn_down, ln2_g, ln2_b):
    depth = w_in.shape[0]
    ns = x_sample.shape[0]
    assert x_sample.shape[1] == 1
    assert x_prompt.shape[1] % CHUNK == 0
    w = _pack_weights(w_in, b_igate, b_fgate, w_conv_mix, mhln_g, w_proj_a, w_proj_b, w_mix_out,
                      ln1_g, ln1_b, w_ffn_up, w_ffn_conv, w_ffn_down, ln2_g, ln2_b)

    sconv_old, sconv_new = cache_sconv[:, :, 0, :], cache_sconv[:, :, 1, :]
    ffn_old, ffn_new = cache_ffn_conv[:, :, 0, :], cache_ffn_conv[:, :, 1, :]

    xp = x_prompt
    xs = x_sample.reshape(ns, D_MODEL)
    outs = [[] for _ in range(9)]
    c_sample = None
    for l in range(depth):
        xp, sp, cp, np_, mp = _mixer_prompt(xp, l, w)
        xp, fp = _ffn_prompt(xp, l, w)

        q, k, v, so, gates, part, sgb, ss = _proj_sample(xs, sconv_old, sconv_new, l, w)
        hm, c_sample, ns_, ms = _state_sample(q, k, v, so, gates, state_mlstm_m, state_mlstm_n,
                                              state_mlstm_C, l, w, c_sample)
        xs, fs = _out_sample(xs, hm, part, sgb, ffn_old, ffn_new, l, w)

        for lst, val in zip(outs, (sp, ss, cp, np_, ns_, mp, ms, fp, fs)):
            lst.append(val)

    sp, ss, cp, np_, ns_, mp, ms, fp, fs = (jnp.stack(o) for o in outs)
    ss = jnp.stack([sconv_new, ss], axis=2)
    fs = jnp.stack([ffn_new, fs], axis=2)
    return (xp, xs.reshape(ns, 1, D_MODEL), sp, ss, cp, c_sample, np_, ns_, mp, ms, fp, fs)
```

```python
import functools

import jax
import jax.numpy as jnp
from jax import lax
from jax.experimental import pallas as pl
from jax.experimental.pallas import tpu as pltpu

F32 = jnp.float32
BF16 = jnp.bfloat16

D_MODEL = 1024
D_CONV = 512
H_M = 4
DH_M = 256
D_M = H_M * DH_M
CONV_W = 3
D_FF = 2816
EPS = 1e-5
DEPTH = 4
ALPHA = (2.0 * DEPTH) ** 0.25
K_SCALE = DH_M ** -0.5

SUBLANES = 8
LANES = 128
VMEM_LIMIT_BYTES = 56 * 1024 * 1024

CHUNK = 128
MIX_CHUNK = 256
MIX_OUT_BLOCKS = 2
TILE_T = 512
FF_CHUNK = 256
N_FF_CHUNKS = D_FF // FF_CHUNK
FFN_ROW_BLOCKS = 2
SEQ_BLOCK = 8
HALO_ROW = SUBLANES - (CONV_W - 1)


def _dot(a, b):
    return jnp.dot(a, b, preferred_element_type=F32)


def _dot_nt(a, b):
    return lax.dot_general(a, b, (((1,), (1,)), ((), ())), preferred_element_type=F32)


def _sigmoid(x):
    return 1.0 / (1.0 + jnp.exp(-x))


def _log_sigmoid(x):
    return -(jnp.maximum(-x, 0.0) + jnp.log1p(jnp.exp(-jnp.abs(x))))


def _layer_norm(y, g, b):
    mu = jnp.mean(y, axis=-1, keepdims=True)
    yc = y - mu
    var = jnp.mean(yc * yc, axis=-1, keepdims=True)
    return yc * lax.rsqrt(var + EPS) * g + b


def _head_norm(h):
    mu = jnp.mean(h, axis=-1, keepdims=True)
    hc = h - mu
    var = jnp.mean(hc * hc, axis=-1, keepdims=True)
    return hc * lax.rsqrt(var + EPS)


def _resident(shape, layer):
    nd = len(shape)
    return pl.BlockSpec((None,) + tuple(shape[1:]),
                        lambda *_: (layer,) + (0,) * (nd - 1),
                        pipeline_mode=pl.Buffered(1))


def _shift_rows(val, prev8, k):
    sub = lax.broadcasted_iota(jnp.int32, prev8.shape, 0)
    rolled = pltpu.roll(val, k, 0)
    top = jnp.where(sub < k, pltpu.roll(prev8, k, 0), rolled[0:SUBLANES, :])
    return jnp.concatenate([top, rolled[SUBLANES:, :]], axis=0)


def _causal_conv3(val, prev8, wc):
    return (_shift_rows(val, prev8, 2) * wc[0:1, :] + _shift_rows(val, prev8, 1) * wc[1:2, :]
            + val * wc[2:3, :])


def _lane_scan(x, op, fill, lane_id):
    shift = 1
    while shift < x.shape[1]:
        x = op(x, jnp.where(lane_id >= shift, pltpu.roll(x, shift, 1), fill))
        shift *= 2
    return x


def _mixer_prompt_kernel(x_ref, wa_ref, wqkvo_ref, wkt_ref, wgt_ref, gbias_ref, wg_ref, wconv_ref,
                         mhg_ref, wpa_ref, wpb_ref, wmix_ref, lng_ref, lnb_ref,
                         x1_ref, sconv_ref, c_ref, n_ref, m_ref,
                         xb_ref, pbuf_ref, merged_ref, q_ref, kt_ref, va_ref, o_ref, hm_ref,
                         caug_ref, mlane_ref, r_ref, w_ref, scol_ref, colmx_ref, colsc_ref,
                         coleinv_ref):
    t = pl.program_id(1)
    tile = x_ref.shape[0]
    L = MIX_CHUNK
    n_chunks = tile // L

    @pl.when(t == 0)
    def _init():
        pbuf_ref[...] = jnp.zeros((SUBLANES, D_CONV), F32)
        caug_ref[...] = jnp.zeros(caug_ref.shape, F32)
        m_ref[...] = jnp.zeros(m_ref.shape, F32)
        mlane_ref[...] = jnp.zeros(mlane_ref.shape, F32)

    x = x_ref[...]
    xb_ref[...] = x.astype(BF16)
    xb = xb_ref[...]

    za = _dot(xb, wa_ref[...])
    bg = za[:, 0:D_CONV]
    p = za[:, D_CONV:2 * D_CONV] * za[:, 2 * D_CONV:3 * D_CONV]
    conv = _causal_conv3(p, pbuf_ref[...], wconv_ref[...])
    halo = p[tile - SUBLANES:tile, :]
    pbuf_ref[...] = halo
    sconv_ref[...] = halo
    ya =_dot((bg * conv).astype(BF16), wpa_ref[...])
    ga = _dot(xb, wg_ref[:, 0:D_MODEL])
    merged_ref[...] = _sigmoid(ga) * ya

    gates = _dot_nt(wgt_ref[...], xb) + gbias_ref[...]
    va_ref[:, :, DH_M:DH_M + LANES] = jnp.ones((H_M, tile, LANES), BF16)
    for h in range(H_M):
        hc_ = slice(h * DH_M, (h + 1) * DH_M)
        q_ref[:, hc_] = _dot(xb_ref[...], wqkvo_ref[:, hc_]).astype(BF16)
        kt_ref[h] = _dot_nt(wkt_ref[hc_, :], xb_ref[...]) * K_SCALE
        va_ref[h, :, 0:DH_M] = _dot(
            xb_ref[...], wqkvo_ref[:, 2 * D_M + h * DH_M:2 * D_M + (h + 1) * DH_M]).astype(BF16)
        o_ref[:, hc_] = _sigmoid(_dot(
            xb_ref[...], wqkvo_ref[:, 3 * D_M + h * DH_M:3 * D_M + (h + 1) * DH_M])) * mhg_ref[:, hc_]

    row_id = lax.broadcasted_iota(jnp.int32, (2 * H_M, L), 0)
    lane_id = lax.broadcasted_iota(jnp.int32, (2 * H_M, L), 1)
    head_rows = lax.broadcasted_iota(jnp.int32, (2 * H_M, 1), 0) < H_M
    m8 = m_ref[:, 0:1]
    mlane = mlane_ref[0:1, :]
    decays = []
    for c in range(n_chunks):
        g = gates[:, c * L:(c + 1) * L]
        csum = _lane_scan(_log_sigmoid(g), jnp.add, 0.0, lane_id)
        b4 = pltpu.roll(csum, H_M, 0)
        r = g - b4
        gmax = _lane_scan(r, jnp.maximum, -jnp.inf, lane_id)
        comb = jnp.where(row_id < H_M, gmax, csum)
        scol = jnp.concatenate([comb, jnp.zeros((LANES - 2 * H_M, L), F32)], axis=0).T
        scol_ref[c * L:(c + 1) * L, :] = scol

        mx_last = jnp.maximum(m8, gmax[:, L - 1:L])
        decays.append(jnp.exp(m8 - mx_last))
        r_ref[c] = r
        w_ref[c] = jnp.exp(r - mx_last)
        m8 = jnp.where(head_rows, b4[:, L - 1:L] + mx_last, 0.0)

        sc = scol_ref[c * L:(c + 1) * L, :]
        mxc = jnp.maximum(mlane, sc)
        colmx_ref[c * L:(c + 1) * L, :] = mxc
        colsc_ref[c * L:(c + 1) * L, :] = jnp.exp(mlane - mxc)
        coleinv_ref[c * L:(c + 1) * L, :] = jnp.exp(-(pltpu.roll(sc, LANES - H_M, 1) + mxc))
        last = sc[L - 1:L, :]
        mlane = pltpu.roll(last, LANES - H_M, 1) + jnp.maximum(mlane, last)
    m_ref[...] = jnp.broadcast_to(m8, m_ref.shape)
    mlane_ref[...] = jnp.broadcast_to(mlane, mlane_ref.shape)

    tri = (lax.broadcasted_iota(jnp.int32, (L, L), 0)
           >= lax.broadcasted_iota(jnp.int32, (L, L), 1))

    for c in range(n_chunks):
        rows = slice(c * L, (c + 1) * L)
        for h in range(H_M):
            hc_ = slice(h * DH_M, (h + 1) * DH_M)
            q = q_ref[rows, hc_]
            kt = kt_ref[h, :, rows]
            va = va_ref[h, rows, :]
            r_row = r_ref[c, h:h + 1, :]
            w_row = w_ref[c, h:h + 1, :]
            mx_col = colmx_ref[rows, h:h + 1]
            sc_col = colsc_ref[rows, h:h + 1]
            einv_col = coleinv_ref[rows, h:h + 1]
            decay = decays[c][h:h + 1, :]

            pmat = jnp.where(tri, jnp.exp(r_row - mx_col), 0.0)
            s = (_dot(q, kt.astype(BF16)) * pmat).astype(BF16)
            ca = caug_ref[h]
            tot = _dot(s, va) + sc_col * _dot(q, ca.astype(BF16))
            dd = jnp.maximum(jnp.abs(tot[:, DH_M:DH_M + LANES]), einv_col)
            hcell = tot[:, 0:DH_M] / jnp.concatenate([dd, dd], axis=1)
            hm = o_ref[rows, hc_] * _head_norm(hcell)
            hm_ref[rows, hc_] = hm.astype(BF16)

            kw = (kt * w_row).astype(BF16)
            caug_ref[h] = decay * ca + _dot(kw, va)

    @pl.when(t == pl.num_programs(1) - 1)
    def _emit_state():
        c_ref[...] = caug_ref[:, :, 0:DH_M]
        n_ref[...] = caug_ref[:, :, DH_M:DH_M + LANES]

    rb = tile // MIX_OUT_BLOCKS
    for r in range(MIX_OUT_BLOCKS):
        rows = slice(r * rb, (r + 1) * rb)
        yb = _dot(hm_ref[rows, :], wpb_ref[...])
        gb = _dot(xb_ref[rows, :], wg_ref[:, D_MODEL:2 * D_MODEL])
        merged = merged_ref[rows, :] + _sigmoid(gb) * yb
        y = ALPHA * x_ref[rows, :] + _dot(merged.astype(BF16), wmix_ref[...])
        x1_ref[rows, :] = _layer_norm(y, lng_ref[...], lnb_ref[...])


def _mixer_prompt(x, layer, w):
    bsz, seq, _ = x.shape
    tile = min(TILE_T, seq)
    n_tiles = seq // tile
    n_chunks = tile // MIX_CHUNK
    row_block = lambda b, t: (b, t, 0)
    per_seq = lambda b, t: (b, 0, 0)
    per_seq4 = lambda b, t: (b, 0, 0, 0)
    names = ("wa", "wqkvo", "wkt", "wgt", "gbias", "wg", "wconv", "mhg", "wpa", "wpb", "wmix",
             "ln1g", "ln1b")
    in_specs = [pl.BlockSpec((None, tile, D_MODEL), row_block)]
    in_specs += [_resident(w[n].shape, layer) for n in names]
    out_shape = (
        jax.ShapeDtypeStruct((bsz, seq, D_MODEL), F32),
        jax.ShapeDtypeStruct((bsz, SUBLANES, D_CONV), F32),
        jax.ShapeDtypeStruct((bsz, H_M, DH_M, DH_M), F32),
        jax.ShapeDtypeStruct((bsz, H_M, DH_M, LANES), F32),
        jax.ShapeDtypeStruct((bsz, SUBLANES, LANES), F32),
    )
    out_specs = (
        pl.BlockSpec((None, tile, D_MODEL), row_block),
        pl.BlockSpec((None, SUBLANES, D_CONV), per_seq),
        pl.BlockSpec((None, H_M, DH_M, DH_M), per_seq4),
        pl.BlockSpec((None, H_M, DH_M, LANES), per_seq4),
        pl.BlockSpec((None, SUBLANES, LANES), per_seq),
    )
    col = pltpu.VMEM((tile, LANES), F32)
    scratch = [
        pltpu.VMEM((tile, D_MODEL), BF16),
        pltpu.VMEM((SUBLANES, D_CONV), F32),
        pltpu.VMEM((tile, D_MODEL), F32),
        pltpu.VMEM((tile, D_M), BF16),
        pltpu.VMEM((H_M, DH_M, tile), F32),
        pltpu.VMEM((H_M, tile, DH_M + LANES), BF16),
        pltpu.VMEM((tile, D_M), F32),
        pltpu.VMEM((tile, D_M), BF16),
        pltpu.VMEM((H_M, DH_M, DH_M + LANES), F32),
        pltpu.VMEM((SUBLANES, LANES), F32),
        pltpu.VMEM((n_chunks, 2 * H_M, MIX_CHUNK), F32),
        pltpu.VMEM((n_chunks, 2 * H_M, MIX_CHUNK), F32),
        col, col, col, col,
    ]
    x1, sconv, c, n, m = pl.pallas_call(
        _mixer_prompt_kernel,
        grid=(bsz, n_tiles),
        in_specs=in_specs,
        out_specs=out_specs,
        out_shape=out_shape,
        scratch_shapes=scratch,
        compiler_params=pltpu.CompilerParams(
            dimension_semantics=("arbitrary", "arbitrary"),
            vmem_limit_bytes=VMEM_LIMIT_BYTES),
        name="mixer_prompt",
    )(x, *[w[n] for n in names])
    return x1, sconv[:, HALO_ROW:, :], c, n[:, :, :, 0], m[:, :H_M, 0]


def _ffn_prompt_kernel(x_ref, wup_ref, wconv_ref, wdown_ref, lng_ref, lnb_ref,
                       x2_ref, halo_ref,
                       xb_ref, acc_ref):
    t = pl.program_id(1)
    tile = x_ref.shape[0]

    @pl.when(t == 0)
    def _init():
        halo_ref[...] = jnp.zeros(halo_ref.shape, F32)

    xb_ref[...] = x_ref[...].astype(BF16)

    rb = tile // FFN_ROW_BLOCKS
    chunk_cols = lambda j: (slice(j * FF_CHUNK, (j + 1) * FF_CHUNK),
                            slice(D_FF + j * FF_CHUNK, D_FF + (j + 1) * FF_CHUNK))

    def up_project(j, r):
        xr = xb_ref[r * rb:(r + 1) * rb, :]
        return tuple(_dot(xr, wup_ref[:, cols]) for cols in chunk_cols(j))

    pending = {(0, r): up_project(0, r) for r in range(FFN_ROW_BLOCKS)}
    for j in range(N_FF_CHUNKS):
        prev = [halo_ref[:, cols] for cols in chunk_cols(j)]
        for r in range(FFN_ROW_BLOCKS):
            rows = slice(r * rb, (r + 1) * rb)
            vals = pending.pop((j, r))
            g, u = (_causal_conv3(v, p8, wconv_ref[:, cols])
                    for v, p8, cols in zip(vals, prev, chunk_cols(j)))
            prev = [v[rb - SUBLANES:rb, :] for v in vals]
            hdn = (g * _sigmoid(g) * u).astype(BF16)
            if j + 1 < N_FF_CHUNKS:
                pending[(j + 1, r)] = up_project(j + 1, r)
            part = _dot(hdn, wdown_ref[j * FF_CHUNK:(j + 1) * FF_CHUNK, :])
            if j == 0:
                acc_ref[rows, :] = part
            else:
                acc_ref[rows, :] += part
        for cols, tail in zip(chunk_cols(j), prev):
            halo_ref[:, cols] = tail

    y = ALPHA * x_ref[...] + acc_ref[...]
    x2_ref[...] = _layer_norm(y, lng_ref[...], lnb_ref[...])


def _ffn_prompt(x, layer, w):
    bsz, seq, _ = x.shape
    tile = min(TILE_T, seq)
    n_tiles = seq // tile
    row_block = lambda b, t: (b, t, 0)
    in_specs = [
        pl.BlockSpec((None, tile, D_MODEL), row_block),
        _resident(w["wup"].shape, layer),
        _resident(w["wfconv"].shape, layer),
        _resident(w["wdown"].shape, layer),
        _resident(w["ln2g"].shape, layer),
        _resident(w["ln2b"].shape, layer),
    ]
    x2, halo = pl.pallas_call(
        _ffn_prompt_kernel,
        grid=(bsz, n_tiles),
        in_specs=in_specs,
        out_specs=(pl.BlockSpec((None, tile, D_MODEL), row_block),
                   pl.BlockSpec((None, SUBLANES, 2 * D_FF), lambda b, t: (b, 0, 0))),
        out_shape=(jax.ShapeDtypeStruct((bsz, seq, D_MODEL), F32),
                   jax.ShapeDtypeStruct((bsz, SUBLANES, 2 * D_FF), F32)),
        scratch_shapes=[
            pltpu.VMEM((tile, D_MODEL), BF16),
            pltpu.VMEM((tile, D_MODEL), F32),
        ],
        compiler_params=pltpu.CompilerParams(
            dimension_semantics=("arbitrary", "arbitrary"),
            vmem_limit_bytes=VMEM_LIMIT_BYTES),
        name="ffn_prompt",
    )(x, w["wup"], w["wfconv"], w["wdown"], w["ln2g"], w["ln2b"])
    return x2, halo[:, HALO_ROW:, :]


def _proj_sample_kernel(x_ref, older_ref, newer_ref, wa_ref, wqkvo_ref, wgc_ref, gbias_ref, wg_ref,
                        wconv_ref, wpa_ref,
                        q_ref, k_ref, v_ref, so_ref, gates_ref, part_ref, sgb_ref, p_ref):
    xb = x_ref[...].astype(BF16)
    za = _dot(xb, wa_ref[...])
    bg = za[:, 0:D_CONV]
    p = za[:, D_CONV:2 * D_CONV] * za[:, 2 * D_CONV:3 * D_CONV]
    wc = wconv_ref[...]
    conv = older_ref[...] * wc[0:1, :] + newer_ref[...] * wc[1:2, :] + p * wc[2:3, :]
    p_ref[...] = p
    ya = _dot((bg * conv).astype(BF16), wpa_ref[...])
    zg = _dot(xb, wg_ref[...])
    part_ref[...] = _sigmoid(zg[:, 0:D_MODEL]) * ya
    sgb_ref[...] = _sigmoid(zg[:, D_MODEL:2 * D_MODEL])
    gz = _dot(xb, wgc_ref[...]) + gbias_ref[...]
    lane = lax.broadcasted_iota(jnp.int32, gz.shape, 1)
    gates_ref[...] = jnp.where(lane < H_M, gz, _log_sigmoid(gz))
    q_ref[...] = _dot(xb, wqkvo_ref[:, 0:D_M])
    k_ref[...] = _dot(xb, wqkvo_ref[:, D_M:2 * D_M]) * K_SCALE
    v_ref[...] = _dot(xb, wqkvo_ref[:, 2 * D_M:3 * D_M])
    so_ref[...] = _sigmoid(_dot(xb, wqkvo_ref[:, 3 * D_M:4 * D_M]))


def _proj_sample(x, older, newer, layer, w):
    ns = x.shape[0]
    whole = lambda shape: pl.BlockSpec(shape, lambda i: (0,) * len(shape))
    act = jax.ShapeDtypeStruct((ns, D_MODEL), F32)
    in_specs = [
        whole((ns, D_MODEL)),
        pl.BlockSpec((None, ns, D_CONV), lambda i: (layer, 0, 0)),
        pl.BlockSpec((None, ns, D_CONV), lambda i: (layer, 0, 0)),
        _resident(w["wa"].shape, layer),
        _resident(w["wqkvo"].shape, layer),
        _resident(w["wgc"].shape, layer),
        _resident(w["gbias_row"].shape, layer),
        _resident(w["wg"].shape, layer),
        _resident(w["wconv"].shape, layer),
        _resident(w["wpa"].shape, layer),
    ]
    out_shape = (act, act, act, act, jax.ShapeDtypeStruct((ns, LANES), F32), act, act,
                 jax.ShapeDtypeStruct((ns, D_CONV), F32))
    out_specs = tuple(whole(s.shape) for s in out_shape)
    return pl.pallas_call(
        _proj_sample_kernel,
        grid=(1,),
        in_specs=in_specs,
        out_specs=out_specs,
        out_shape=out_shape,
        compiler_params=pltpu.CompilerParams(
            dimension_semantics=("arbitrary",), vmem_limit_bytes=VMEM_LIMIT_BYTES),
        name="proj_sample",
    )(x, older, newer, w["wa"], w["wqkvo"], w["wgc"], w["gbias_row"], w["wg"], w["wconv"], w["wpa"])


def _state_sample_kernel(q_ref, k_ref, v_ref, so_ref, gates_ref, m0_ref, n0_ref, c0_ref, mhg_ref,
                         *rest):
    hm_ref, c_ref, n_ref, m_ref, qc_ref = rest[-5:]
    nb = q_ref.shape[0]
    gates = gates_ref[...]
    zpad = jnp.zeros((LANES - nb, DH_M), F32)
    for h in range(H_M):
        cols = slice(h * DH_M, (h + 1) * DH_M)
        q = q_ref[:, cols]
        k = k_ref[:, cols]
        v = v_ref[:, cols]
        li = gates[:, h:h + 1]
        lf = gates[:, H_M + h:H_M + h + 1]
        m0 = m0_ref[:, h:h + 1]
        n0 = n0_ref[:, h, :]
        m_new = jnp.maximum(lf + m0, li)
        decay = jnp.exp(lf + m0 - m_new)
        wgt = jnp.exp(li - m_new)
        kw = k * wgt
        qb = q.astype(BF16)
        kw_t = jnp.concatenate([kw, zpad], axis=0).T
        v_pad = jnp.concatenate([v, zpad], axis=0).astype(BF16)
        seq_lane = lax.broadcasted_iota(jnp.int32, kw_t.shape, 1)
        for i in range(nb):
            c0 = c0_ref[i, h]
            qc_ref[i:i + 1, :] = _dot(qb, c0.astype(BF16))[i:i + 1, :]
            kw_i = jnp.where(seq_lane == i, kw_t, 0.0).astype(BF16)
            c_ref[i, h] = decay[i:i + 1, :] * c0 + _dot(kw_i, v_pad)
        s = jnp.sum(q * k, axis=-1, keepdims=True) * wgt
        numer = s * v + decay * qc_ref[...]
        den = s + decay * jnp.sum(q * n0, axis=-1, keepdims=True)
        hc = numer / jnp.maximum(jnp.abs(den), jnp.exp(-m_new))
        hm_ref[:, cols] = so_ref[:, cols] * (_head_norm(hc) * mhg_ref[:, cols])
        n_ref[:, h, :] = decay * n0 + kw
        m_ref[:, h:h + 1] = m_new


def _state_sample(q, k, v, so, gates, m0, n0, c0, layer, w, c_all):
    depth = c0.shape[0]
    ns = q.shape[0]
    nb = min(SEQ_BLOCK, ns)
    rows = lambda i: (i, 0)
    act_spec = pl.BlockSpec((nb, D_MODEL), rows)
    layer_slab = lambda i: (layer, i, 0, 0, 0)
    in_specs = [
        act_spec, act_spec, act_spec, act_spec,
        pl.BlockSpec((nb, LANES), rows),
        pl.BlockSpec((None, nb, H_M), lambda i: (layer, i, 0)),
        pl.BlockSpec((None, nb, H_M, DH_M), lambda i: (layer, i, 0, 0)),
        pl.BlockSpec((None, nb, H_M, DH_M, DH_M), layer_slab),
        pl.BlockSpec((None, 1, D_M), lambda i: (layer, 0, 0)),
    ]
    args = [q, k, v, so, gates, m0, n0, c0, w["mhg"]]
    aliases = {}
    if c_all is not None:
        in_specs.append(pl.BlockSpec(memory_space=pl.ANY))
        args.append(c_all)
        aliases = {len(args) - 1: 1}
    out_shape = (
        jax.ShapeDtypeStruct((ns, D_M), F32),
        jax.ShapeDtypeStruct((depth, ns, H_M, DH_M, DH_M), F32),
        jax.ShapeDtypeStruct((ns, H_M, DH_M), F32),
        jax.ShapeDtypeStruct((ns, H_M), F32),
    )
    out_specs = (
        act_spec,
        pl.BlockSpec((None, nb, H_M, DH_M, DH_M), layer_slab),
        pl.BlockSpec((nb, H_M, DH_M), lambda i: (i, 0, 0)),
        pl.BlockSpec((nb, H_M), rows),
    )
    return pl.pallas_call(
        _state_sample_kernel,
        grid=(ns // nb,),
        in_specs=in_specs,
        out_specs=out_specs,
        out_shape=out_shape,
        input_output_aliases=aliases,
        scratch_shapes=[pltpu.VMEM((nb, DH_M), F32)],
        compiler_params=pltpu.CompilerParams(
            dimension_semantics=("arbitrary",), vmem_limit_bytes=VMEM_LIMIT_BYTES),
        name="state_sample",
    )(*args)


def _out_sample_kernel(x_ref, hm_ref, part_ref, sgb_ref, older_ref, newer_ref, wpb_ref, wmix_ref,
                       ln1g_ref, ln1b_ref, wup_ref, wconv_ref, wdown_ref, ln2g_ref, ln2b_ref,
                       x2_ref, up_ref,
                       x1_ref, xb_ref, gate_ref, acc_ref):
    k = pl.program_id(0)

    @pl.when(k == 0)
    def _mix():
        yb = _dot(hm_ref[...].astype(BF16), wpb_ref[...])
        merged = part_ref[...] + sgb_ref[...] * yb
        y = ALPHA * x_ref[...] + _dot(merged.astype(BF16), wmix_ref[...])
        x1 = _layer_norm(y, ln1g_ref[...], ln1b_ref[...])
        x1_ref[...] = x1
        xb_ref[...] = x1.astype(BF16)
        acc_ref[...] = jnp.zeros(acc_ref.shape, F32)

    val = _dot(xb_ref[...], wup_ref[...])
    wc = wconv_ref[...]
    conv = older_ref[...] * wc[0:1, :] + newer_ref[...] * wc[1:2, :] + val * wc[2:3, :]
    up_ref[...] = val

    @pl.when(k < N_FF_CHUNKS)
    def _gate():
        gate_ref[k] = conv * _sigmoid(conv)

    @pl.when(k >= N_FF_CHUNKS)
    def _down():
        hdn = (gate_ref[k - N_FF_CHUNKS] * conv).astype(BF16)
        acc_ref[...] += _dot(hdn, wdown_ref[...])

    @pl.when(k == 2 * N_FF_CHUNKS - 1)
    def _finish():
        x2_ref[...] = _layer_norm(ALPHA * x1_ref[...] + acc_ref[...], ln2g_ref[...], ln2b_ref[...])


def _out_sample(x, hm, part, sgb, older, newer, layer, w):
    ns = x.shape[0]
    act = pl.BlockSpec((ns, D_MODEL), lambda k: (0, 0))
    col_chunk = lambda k: (layer, 0, k)
    in_specs = [
        act, act, act, act,
        pl.BlockSpec((None, ns, FF_CHUNK), col_chunk),
        pl.BlockSpec((None, ns, FF_CHUNK), col_chunk),
        _resident(w["wpb"].shape, layer),
        _resident(w["wmix"].shape, layer),
        _resident(w["ln1g"].shape, layer),
        _resident(w["ln1b"].shape, layer),
        pl.BlockSpec((None, D_MODEL, FF_CHUNK), col_chunk),
        pl.BlockSpec((None, CONV_W, FF_CHUNK), col_chunk),
        pl.BlockSpec((None, FF_CHUNK, D_MODEL),
                     lambda k: (layer, jnp.maximum(k - N_FF_CHUNKS, 0), 0)),
        _resident(w["ln2g"].shape, layer),
        _resident(w["ln2b"].shape, layer),
    ]
    names = ("wpb", "wmix", "ln1g", "ln1b", "wup", "wfconv", "wdown", "ln2g", "ln2b")
    return pl.pallas_call(
        _out_sample_kernel,
        grid=(2 * N_FF_CHUNKS,),
        in_specs=in_specs,
        out_specs=(act, pl.BlockSpec((ns, FF_CHUNK), lambda k: (0, k))),
        out_shape=(jax.ShapeDtypeStruct((ns, D_MODEL), F32),
                   jax.ShapeDtypeStruct((ns, 2 * D_FF), F32)),
        scratch_shapes=[
            pltpu.VMEM((ns, D_MODEL), F32),
            pltpu.VMEM((ns, D_MODEL), BF16),
            pltpu.VMEM((N_FF_CHUNKS, ns, FF_CHUNK), F32),
            pltpu.VMEM((ns, D_MODEL), F32),
        ],
        compiler_params=pltpu.CompilerParams(
            dimension_semantics=("arbitrary",), vmem_limit_bytes=VMEM_LIMIT_BYTES),
        name="out_sample",
    )(x, hm, part, sgb, older, newer, *[w[n] for n in names])


def _transpose_cast_kernel(w_ref, o_ref):
    o_ref[...] = w_ref[...].T.astype(BF16)


def _k_weights_transposed(w_k):
    depth = w_k.shape[0]
    return pl.pallas_call(
        _transpose_cast_kernel,
        grid=(depth, H_M),
        in_specs=[pl.BlockSpec((None, D_MODEL, DH_M), lambda l, h: (l, 0, h))],
        out_specs=pl.BlockSpec((None, DH_M, D_MODEL), lambda l, h: (l, h, 0)),
        out_shape=jax.ShapeDtypeStruct((depth, D_M, D_MODEL), BF16),
        compiler_params=pltpu.CompilerParams(dimension_semantics=("arbitrary", "arbitrary")),
        name="k_weights_transposed",
    )(w_k)


def _pack_weights(w_in, b_igate, b_fgate, w_conv_mix, mhln_g, w_proj_a, w_proj_b, w_mix_out,
                  ln1_g, ln1_b, w_ffn_up, w_ffn_conv, w_ffn_down, ln2_g, ln2_b):
    depth = w_in.shape[0]
    o_a = 3 * D_CONV
    o_gate = o_a + 4 * D_M
    o_g = o_gate + 2 * H_M
    wgate = w_in[:, :, o_gate:o_g]
    gbias = jnp.concatenate([b_igate, b_fgate], axis=-1)
    wgc = jnp.pad(wgate, ((0, 0), (0, 0), (0, LANES - 2 * H_M)))
    gbias_row = jnp.pad(gbias, ((0, 0), (0, LANES - 2 * H_M)))[:, None, :]

    row = lambda a: a[:, None, :]
    return {
        "wa": w_in[:, :, 0:o_a].astype(BF16),
        "wqkvo": w_in[:, :, o_a:o_gate].astype(BF16),
        "wkt": _k_weights_transposed(w_in[:, :, o_a + D_M:o_a + 2 * D_M]),
        "wgt": wgate.transpose(0, 2, 1).astype(BF16),
        "gbias": gbias[:, :, None],
        "wgc": wgc.astype(BF16),
        "gbias_row": gbias_row,
        "wg": w_in[:, :, o_g:o_g + 2 * D_MODEL].astype(BF16),
        "wconv": w_conv_mix,
        "mhg": row(mhln_g),
        "wpa": w_proj_a.astype(BF16),
        "wpb": w_proj_b.astype(BF16),
        "wmix": w_mix_out.astype(BF16),
        "ln1g": row(ln1_g), "ln1b": row(ln1_b),
        "wup": w_ffn_up.astype(BF16),
        "wfconv": w_ffn_conv,
        "wdown": w_ffn_down.astype(BF16),
        "ln2g": row(ln2_g), "ln2b": row(ln2_b),
    }


def kernel(x_prompt, x_sample, cache_sconv, state_mlstm_C, state_mlstm_n, state_mlstm_m, cache_ffn_conv, w_in, b_igate, b_fgate, w_conv_mix, mhln_g, w_proj_a, w_proj_b, w_mix_out, ln1_g, ln1_b, w_ffn_up, w_ffn_conv, w_ffn_down, ln2_g, ln2_b):
    depth = w_in.shape[0]
    ns = x_sample.shape[0]
    assert x_sample.shape[1] == 1
    assert x_prompt.shape[1] % CHUNK == 0
    w = _pack_weights(w_in, b_igate, b_fgate, w_conv_mix, mhln_g, w_proj_a, w_proj_b, w_mix_out,
                      ln1_g, ln1_b, w_ffn_up, w_ffn_conv, w_ffn_down, ln2_g, ln2_b)

    sconv_old, sconv_new = cache_sconv[:, :, 0, :], cache_sconv[:, :, 1, :]
    ffn_old, ffn_new = cache_ffn_conv[:, :, 0, :], cache_ffn_conv[:, :, 1, :]

    xp = x_prompt
    xs = x_sample.reshape(ns, D_MODEL)
    outs = [[] for _ in range(9)]
    c_sample = None
    for l in range(depth):
        xp, sp, cp, np_, mp = _mixer_prompt(xp, l, w)
        xp, fp = _ffn_prompt(xp, l, w)

        q, k, v, so, gates, part, sgb, ss = _proj_sample(xs, sconv_old, sconv_new, l, w)
        hm, c_sample, ns_, ms = _state_sample(q, k, v, so, gates, state_mlstm_m, state_mlstm_n,
                                              state_mlstm_C, l, w, c_sample)
        xs, fs = _out_sample(xs, hm, part, sgb, ffn_old, ffn_new, l, w)

        for lst, val in zip(outs, (sp, ss, cp, np_, ns_, mp, ms, fp, fs)):
            lst.append(val)

    sp, ss, cp, np_, ns_, mp, ms, fp, fs = (jnp.stack(o) for o in outs)
    ss = jnp.stack([sconv_new, ss], axis=2)
    fs = jnp.stack([ffn_new, fs], axis=2)
    return (xp, xs.reshape(ns, 1, D_MODEL), sp, ss, cp, c_sample, np_, ns_, mp, ms, fp, fs)
```

```python
import functools

import jax
import jax.numpy as jnp
from jax import lax
from jax.experimental import pallas as pl
from jax.experimental.pallas import tpu as pltpu

F32 = jnp.float32
BF16 = jnp.bfloat16

D_MODEL = 1024
D_CONV = 512
H_M = 4
DH_M = 256
D_M = H_M * DH_M
CONV_W = 3
D_FF = 2816
EPS = 1e-5
DEPTH = 4
ALPHA = (2.0 * DEPTH) ** 0.25
K_SCALE = DH_M ** -0.5
O_QKVO = 3 * D_CONV

SUBLANES = 8
LANES = 128
VMEM_LIMIT_BYTES = 56 * 1024 * 1024

CHUNK = 128
MIX_CHUNK = 256
MIX_OUT_BLOCKS = 2
TILE_T = 512
FF_CHUNK = 256
N_FF_CHUNKS = D_FF // FF_CHUNK
FFN_ROW_BLOCKS = 2
SEQ_BLOCK = 8
HALO_ROW = SUBLANES - (CONV_W - 1)


def _dot(a, b):
    return jnp.dot(a, b, preferred_element_type=F32)


def _dot_nt(a, b):
    return lax.dot_general(a, b, (((1,), (1,)), ((), ())), preferred_element_type=F32)


def _sigmoid(x):
    return 1.0 / (1.0 + jnp.exp(-x))


def _log_sigmoid(x):
    return -(jnp.maximum(-x, 0.0) + jnp.log1p(jnp.exp(-jnp.abs(x))))


def _layer_norm(y, g, b):
    mu = jnp.mean(y, axis=-1, keepdims=True)
    yc = y - mu
    var = jnp.mean(yc * yc, axis=-1, keepdims=True)
    return yc * lax.rsqrt(var + EPS) * g + b


def _head_norm(h):
    mu = jnp.mean(h, axis=-1, keepdims=True)
    hc = h - mu
    var = jnp.mean(hc * hc, axis=-1, keepdims=True)
    return hc * lax.rsqrt(var + EPS)


def _resident(shape, layer):
    nd = len(shape)
    return pl.BlockSpec((None,) + tuple(shape[1:]),
                        lambda *_: (layer,) + (0,) * (nd - 1),
                        pipeline_mode=pl.Buffered(1))


def _shift_rows(val, prev8, k):
    sub = lax.broadcasted_iota(jnp.int32, prev8.shape, 0)
    rolled = pltpu.roll(val, k, 0)
    top = jnp.where(sub < k, pltpu.roll(prev8, k, 0), rolled[0:SUBLANES, :])
    return jnp.concatenate([top, rolled[SUBLANES:, :]], axis=0)


def _causal_conv3(val, prev8, wc):
    return (_shift_rows(val, prev8, 2) * wc[0:1, :] + _shift_rows(val, prev8, 1) * wc[1:2, :]
            + val * wc[2:3, :])


def _lane_scan(x, op, fill, lane_id):
    shift = 1
    while shift < x.shape[1]:
        x = op(x, jnp.where(lane_id >= shift, pltpu.roll(x, shift, 1), fill))
        shift *= 2
    return x


def _mixer_prompt_kernel(x_ref, win_ref, wgt_ref, gbias_ref, wg_ref, wconv_ref,
                         mhg_ref, wpa_ref, wpb_ref, wmix_ref, lng_ref, lnb_ref,
                         x1_ref, sconv_ref, c_ref, n_ref, m_ref,
                         xb_ref, pbuf_ref, merged_ref, q_ref, kt_ref, va_ref, o_ref, hm_ref,
                         caug_ref, mlane_ref, r_ref, w_ref, scol_ref, colmx_ref, colsc_ref,
                         coleinv_ref):
    t = pl.program_id(1)
    tile = x_ref.shape[0]
    L = MIX_CHUNK
    n_chunks = tile // L

    @pl.when(t == 0)
    def _init():
        pbuf_ref[...] = jnp.zeros((SUBLANES, D_CONV), F32)
        caug_ref[...] = jnp.zeros(caug_ref.shape, F32)
        m_ref[...] = jnp.zeros(m_ref.shape, F32)
        mlane_ref[...] = jnp.zeros(mlane_ref.shape, F32)

    x = x_ref[...]
    xb_ref[...] = x.astype(BF16)
    xb = xb_ref[...]

    za = _dot_nt(xb, win_ref[0:O_QKVO, :])
    bg = za[:, 0:D_CONV]
    p = za[:, D_CONV:2 * D_CONV] * za[:, 2 * D_CONV:3 * D_CONV]
    conv = _causal_conv3(p, pbuf_ref[...], wconv_ref[...])
    halo = p[tile - SUBLANES:tile, :]
    pbuf_ref[...] = halo
    sconv_ref[...] = halo
    ya =_dot((bg * conv).astype(BF16), wpa_ref[...])
    ga = _dot_nt(xb, wg_ref[0:D_MODEL, :])
    merged_ref[...] = _sigmoid(ga) * ya

    gates = _dot_nt(wgt_ref[...], xb) + gbias_ref[...]
    va_ref[:, :, DH_M:DH_M + LANES] = jnp.ones((H_M, tile, LANES), BF16)
    for h in range(H_M):
        hc_ = slice(h * DH_M, (h + 1) * DH_M)
        w_rows = lambda i: win_ref[O_QKVO + i * D_M + h * DH_M:
                                   O_QKVO + i * D_M + (h + 1) * DH_M, :]
        q_ref[:, hc_] = _dot_nt(xb_ref[...], w_rows(0)).astype(BF16)
        kt_ref[h] = _dot_nt(w_rows(1), xb_ref[...]) * K_SCALE
        va_ref[h, :, 0:DH_M] = _dot_nt(xb_ref[...], w_rows(2)).astype(BF16)
        o_ref[:, hc_] = _sigmoid(_dot_nt(xb_ref[...], w_rows(3))) * mhg_ref[:, hc_]

    row_id = lax.broadcasted_iota(jnp.int32, (2 * H_M, L), 0)
    lane_id = lax.broadcasted_iota(jnp.int32, (2 * H_M, L), 1)
    head_rows = lax.broadcasted_iota(jnp.int32, (2 * H_M, 1), 0) < H_M
    m8 = m_ref[:, 0:1]
    mlane = mlane_ref[0:1, :]
    decays = []
    for c in range(n_chunks):
        g = gates[:, c * L:(c + 1) * L]
        csum = _lane_scan(_log_sigmoid(g), jnp.add, 0.0, lane_id)
        b4 = pltpu.roll(csum, H_M, 0)
        r = g - b4
        gmax = _lane_scan(r, jnp.maximum, -jnp.inf, lane_id)
        comb = jnp.where(row_id < H_M, gmax, csum)
        scol = jnp.concatenate([comb, jnp.zeros((LANES - 2 * H_M, L), F32)], axis=0).T
        scol_ref[c * L:(c + 1) * L, :] = scol

        mx_last = jnp.maximum(m8, gmax[:, L - 1:L])
        decays.append(jnp.exp(m8 - mx_last))
        r_ref[c] = r
        w_ref[c] = jnp.exp(r - mx_last)
        m8 = jnp.where(head_rows, b4[:, L - 1:L] + mx_last, 0.0)

        sc = scol_ref[c * L:(c + 1) * L, :]
        mxc = jnp.maximum(mlane, sc)
        colmx_ref[c * L:(c + 1) * L, :] = mxc
        colsc_ref[c * L:(c + 1) * L, :] = jnp.exp(mlane - mxc)
        coleinv_ref[c * L:(c + 1) * L, :] = jnp.exp(-(pltpu.roll(sc, LANES - H_M, 1) + mxc))
        last = sc[L - 1:L, :]
        mlane = pltpu.roll(last, LANES - H_M, 1) + jnp.maximum(mlane, last)
    m_ref[...] = jnp.broadcast_to(m8, m_ref.shape)
    mlane_ref[...] = jnp.broadcast_to(mlane, mlane_ref.shape)

    tri = (lax.broadcasted_iota(jnp.int32, (L, L), 0)
           >= lax.broadcasted_iota(jnp.int32, (L, L), 1))

    for c in range(n_chunks):
        rows = slice(c * L, (c + 1) * L)
        for h in range(H_M):
            hc_ = slice(h * DH_M, (h + 1) * DH_M)
            q = q_ref[rows, hc_]
            kt = kt_ref[h, :, rows]
            va = va_ref[h, rows, :]
            r_row = r_ref[c, h:h + 1, :]
            w_row = w_ref[c, h:h + 1, :]
            mx_col = colmx_ref[rows, h:h + 1]
            sc_col = colsc_ref[rows, h:h + 1]
            einv_col = coleinv_ref[rows, h:h + 1]
            decay = decays[c][h:h + 1, :]

            pmat = jnp.where(tri, jnp.exp(r_row - mx_col), 0.0)
            s = (_dot(q, kt.astype(BF16)) * pmat).astype(BF16)
            ca = caug_ref[h]
            tot = _dot(s, va) + sc_col * _dot(q, ca.astype(BF16))
            dd = jnp.maximum(jnp.abs(tot[:, DH_M:DH_M + LANES]), einv_col)
            hcell = tot[:, 0:DH_M] / jnp.concatenate([dd, dd], axis=1)
            hm = o_ref[rows, hc_] * _head_norm(hcell)
            hm_ref[rows, hc_] = hm.astype(BF16)

            kw = (kt * w_row).astype(BF16)
            caug_ref[h] = decay * ca + _dot(kw, va)

    @pl.when(t == pl.num_programs(1) - 1)
    def _emit_state():
        c_ref[...] = caug_ref[:, :, 0:DH_M]
        n_ref[...] = caug_ref[:, :, DH_M:DH_M + LANES]

    rb = tile // MIX_OUT_BLOCKS
    for r in range(MIX_OUT_BLOCKS):
        rows = slice(r * rb, (r + 1) * rb)
        yb = _dot(hm_ref[rows, :], wpb_ref[...])
        gb = _dot_nt(xb_ref[rows, :], wg_ref[D_MODEL:2 * D_MODEL, :])
        merged = merged_ref[rows, :] + _sigmoid(gb) * yb
        y = ALPHA * x_ref[rows, :] + _dot(merged.astype(BF16), wmix_ref[...])
        x1_ref[rows, :] = _layer_norm(y, lng_ref[...], lnb_ref[...])


def _mixer_prompt(x, layer, w):
    bsz, seq, _ = x.shape
    tile = min(TILE_T, seq)
    n_tiles = seq // tile
    n_chunks = tile // MIX_CHUNK
    row_block = lambda b, t: (b, t, 0)
    per_seq = lambda b, t: (b, 0, 0)
    per_seq4 = lambda b, t: (b, 0, 0, 0)
    names = ("win", "wgt", "gbias", "wg", "wconv", "mhg", "wpa", "wpb", "wmix",
             "ln1g", "ln1b")
    in_specs = [pl.BlockSpec((None, tile, D_MODEL), row_block)]
    in_specs += [_resident(w[n].shape, layer) for n in names]
    out_shape = (
        jax.ShapeDtypeStruct((bsz, seq, D_MODEL), F32),
        jax.ShapeDtypeStruct((bsz, SUBLANES, D_CONV), F32),
        jax.ShapeDtypeStruct((bsz, H_M, DH_M, DH_M), F32),
        jax.ShapeDtypeStruct((bsz, H_M, DH_M, LANES), F32),
        jax.ShapeDtypeStruct((bsz, SUBLANES, LANES), F32),
    )
    out_specs = (
        pl.BlockSpec((None, tile, D_MODEL), row_block),
        pl.BlockSpec((None, SUBLANES, D_CONV), per_seq),
        pl.BlockSpec((None, H_M, DH_M, DH_M), per_seq4),
        pl.BlockSpec((None, H_M, DH_M, LANES), per_seq4),
        pl.BlockSpec((None, SUBLANES, LANES), per_seq),
    )
    col = pltpu.VMEM((tile, LANES), F32)
    scratch = [
        pltpu.VMEM((tile, D_MODEL), BF16),
        pltpu.VMEM((SUBLANES, D_CONV), F32),
        pltpu.VMEM((tile, D_MODEL), F32),
        pltpu.VMEM((tile, D_M), BF16),
        pltpu.VMEM((H_M, DH_M, tile), F32),
        pltpu.VMEM((H_M, tile, DH_M + LANES), BF16),
        pltpu.VMEM((tile, D_M), F32),
        pltpu.VMEM((tile, D_M), BF16),
        pltpu.VMEM((H_M, DH_M, DH_M + LANES), F32),
        pltpu.VMEM((SUBLANES, LANES), F32),
        pltpu.VMEM((n_chunks, 2 * H_M, MIX_CHUNK), F32),
        pltpu.VMEM((n_chunks, 2 * H_M, MIX_CHUNK), F32),
        col, col, col, col,
    ]
    x1, sconv, c, n, m = pl.pallas_call(
        _mixer_prompt_kernel,
        grid=(bsz, n_tiles),
        in_specs=in_specs,
        out_specs=out_specs,
        out_shape=out_shape,
        scratch_shapes=scratch,
        compiler_params=pltpu.CompilerParams(
            dimension_semantics=("arbitrary", "arbitrary"),
            vmem_limit_bytes=VMEM_LIMIT_BYTES),
        name="mixer_prompt",
    )(x, *[w[n] for n in names])
    return x1, sconv[:, HALO_ROW:, :], c, n[:, :, :, 0], m[:, :H_M, 0]


def _ffn_prompt_kernel(x_ref, wup_ref, wconv_ref, wdown_ref, lng_ref, lnb_ref,
                       x2_ref, halo_ref,
                       xb_ref, acc_ref):
    t = pl.program_id(1)
    tile = x_ref.shape[0]

    @pl.when(t == 0)
    def _init():
        halo_ref[...] = jnp.zeros(halo_ref.shape, F32)

    xb_ref[...] = x_ref[...].astype(BF16)

    rb = tile // FFN_ROW_BLOCKS
    chunk_cols = lambda j: (slice(j * FF_CHUNK, (j + 1) * FF_CHUNK),
                            slice(D_FF + j * FF_CHUNK, D_FF + (j + 1) * FF_CHUNK))

    def up_project(j, r):
        xr = xb_ref[r * rb:(r + 1) * rb, :]
        return tuple(_dot(xr, wup_ref[:, cols]) for cols in chunk_cols(j))

    pending = {(0, r): up_project(0, r) for r in range(FFN_ROW_BLOCKS)}
    for j in range(N_FF_CHUNKS):
        prev = [halo_ref[:, cols] for cols in chunk_cols(j)]
        for r in range(FFN_ROW_BLOCKS):
            rows = slice(r * rb, (r + 1) * rb)
            vals = pending.pop((j, r))
            g, u = (_causal_conv3(v, p8, wconv_ref[:, cols])
                    for v, p8, cols in zip(vals, prev, chunk_cols(j)))
            prev = [v[rb - SUBLANES:rb, :] for v in vals]
            hdn = (g * _sigmoid(g) * u).astype(BF16)
            if j + 1 < N_FF_CHUNKS:
                pending[(j + 1, r)] = up_project(j + 1, r)
            part = _dot(hdn, wdown_ref[j * FF_CHUNK:(j + 1) * FF_CHUNK, :])
            if j == 0:
                acc_ref[rows, :] = part
            else:
                acc_ref[rows, :] += part
        for cols, tail in zip(chunk_cols(j), prev):
            halo_ref[:, cols] = tail

    y = ALPHA * x_ref[...] + acc_ref[...]
    x2_ref[...] = _layer_norm(y, lng_ref[...], lnb_ref[...])


def _ffn_prompt(x, layer, w):
    bsz, seq, _ = x.shape
    tile = min(TILE_T, seq)
    n_tiles = seq // tile
    row_block = lambda b, t: (b, t, 0)
    in_specs = [
        pl.BlockSpec((None, tile, D_MODEL), row_block),
        _resident(w["wup"].shape, layer),
        _resident(w["wfconv"].shape, layer),
        _resident(w["wdown"].shape, layer),
        _resident(w["ln2g"].shape, layer),
        _resident(w["ln2b"].shape, layer),
    ]
    x2, halo = pl.pallas_call(
        _ffn_prompt_kernel,
        grid=(bsz, n_tiles),
        in_specs=in_specs,
        out_specs=(pl.BlockSpec((None, tile, D_MODEL), row_block),
                   pl.BlockSpec((None, SUBLANES, 2 * D_FF), lambda b, t: (b, 0, 0))),
        out_shape=(jax.ShapeDtypeStruct((bsz, seq, D_MODEL), F32),
                   jax.ShapeDtypeStruct((bsz, SUBLANES, 2 * D_FF), F32)),
        scratch_shapes=[
            pltpu.VMEM((tile, D_MODEL), BF16),
            pltpu.VMEM((tile, D_MODEL), F32),
        ],
        compiler_params=pltpu.CompilerParams(
            dimension_semantics=("arbitrary", "arbitrary"),
            vmem_limit_bytes=VMEM_LIMIT_BYTES),
        name="ffn_prompt",
    )(x, w["wup"], w["wfconv"], w["wdown"], w["ln2g"], w["ln2b"])
    return x2, halo[:, HALO_ROW:, :]


def _proj_sample_kernel(x_ref, older_ref, newer_ref, win_ref, wgc_ref, gbias_ref, wg_ref,
                        wconv_ref, wpa_ref,
                        q_ref, k_ref, v_ref, so_ref, gates_ref, part_ref, sgb_ref, p_ref):
    xb = x_ref[...].astype(BF16)
    za = _dot_nt(xb, win_ref[0:O_QKVO, :])
    bg = za[:, 0:D_CONV]
    p = za[:, D_CONV:2 * D_CONV] * za[:, 2 * D_CONV:3 * D_CONV]
    wc = wconv_ref[...]
    conv = older_ref[...] * wc[0:1, :] + newer_ref[...] * wc[1:2, :] + p * wc[2:3, :]
    p_ref[...] = p
    ya = _dot((bg * conv).astype(BF16), wpa_ref[...])
    zg = _dot_nt(xb, wg_ref[...])
    part_ref[...] = _sigmoid(zg[:, 0:D_MODEL]) * ya
    sgb_ref[...] = _sigmoid(zg[:, D_MODEL:2 * D_MODEL])
    gz = _dot_nt(xb, wgc_ref[...]) + gbias_ref[...]
    lane = lax.broadcasted_iota(jnp.int32, gz.shape, 1)
    gates_ref[...] = jnp.where(lane < H_M, gz, _log_sigmoid(gz))
    w_rows = lambda i: win_ref[O_QKVO + i * D_M:O_QKVO + (i + 1) * D_M, :]
    q_ref[...] = _dot_nt(xb, w_rows(0))
    k_ref[...] = _dot_nt(xb, w_rows(1)) * K_SCALE
    v_ref[...] = _dot_nt(xb, w_rows(2))
    so_ref[...] = _sigmoid(_dot_nt(xb, w_rows(3)))


def _proj_sample(x, older, newer, layer, w):
    ns = x.shape[0]
    whole = lambda shape: pl.BlockSpec(shape, lambda i: (0,) * len(shape))
    act = jax.ShapeDtypeStruct((ns, D_MODEL), F32)
    in_specs = [
        whole((ns, D_MODEL)),
        pl.BlockSpec((None, ns, D_CONV), lambda i: (layer, 0, 0)),
        pl.BlockSpec((None, ns, D_CONV), lambda i: (layer, 0, 0)),
        _resident(w["win"].shape, layer),
        _resident(w["wgc"].shape, layer),
        _resident(w["gbias_row"].shape, layer),
        _resident(w["wg"].shape, layer),
        _resident(w["wconv"].shape, layer),
        _resident(w["wpa"].shape, layer),
    ]
    out_shape = (act, act, act, act, jax.ShapeDtypeStruct((ns, LANES), F32), act, act,
                 jax.ShapeDtypeStruct((ns, D_CONV), F32))
    out_specs = tuple(whole(s.shape) for s in out_shape)
    return pl.pallas_call(
        _proj_sample_kernel,
        grid=(1,),
        in_specs=in_specs,
        out_specs=out_specs,
        out_shape=out_shape,
        compiler_params=pltpu.CompilerParams(
            dimension_semantics=("arbitrary",), vmem_limit_bytes=VMEM_LIMIT_BYTES),
        name="proj_sample",
    )(x, older, newer, w["win"], w["wgc"], w["gbias_row"], w["wg"], w["wconv"], w["wpa"])


def _state_sample_kernel(q_ref, k_ref, v_ref, so_ref, gates_ref, m0_ref, n0_ref, c0_ref, mhg_ref,
                         *rest):
    hm_ref, c_ref, n_ref, m_ref, qc_ref = rest[-5:]
    nb = q_ref.shape[0]
    gates = gates_ref[...]
    zpad = jnp.zeros((LANES - nb, DH_M), F32)
    for h in range(H_M):
        cols = slice(h * DH_M, (h + 1) * DH_M)
        q = q_ref[:, cols]
        k = k_ref[:, cols]
        v = v_ref[:, cols]
        li = gates[:, h:h + 1]
        lf = gates[:, H_M + h:H_M + h + 1]
        m0 = m0_ref[:, h:h + 1]
        n0 = n0_ref[:, h, :]
        m_new = jnp.maximum(lf + m0, li)
        decay = jnp.exp(lf + m0 - m_new)
        wgt = jnp.exp(li - m_new)
        kw = k * wgt
        qb = q.astype(BF16)
        kw_t = jnp.concatenate([kw, zpad], axis=0).T
        v_pad = jnp.concatenate([v, zpad], axis=0).astype(BF16)
        seq_lane = lax.broadcasted_iota(jnp.int32, kw_t.shape, 1)
        for i in range(nb):
            c0 = c0_ref[i, h]
            qc_ref[i:i + 1, :] = _dot(qb, c0.astype(BF16))[i:i + 1, :]
            kw_i = jnp.where(seq_lane == i, kw_t, 0.0).astype(BF16)
            c_ref[i, h] = decay[i:i + 1, :] * c0 + _dot(kw_i, v_pad)
        s = jnp.sum(q * k, axis=-1, keepdims=True) * wgt
        numer = s * v + decay * qc_ref[...]
        den = s + decay * jnp.sum(q * n0, axis=-1, keepdims=True)
        hc = numer / jnp.maximum(jnp.abs(den), jnp.exp(-m_new))
        hm_ref[:, cols] = so_ref[:, cols] * (_head_norm(hc) * mhg_ref[:, cols])
        n_ref[:, h, :] = decay * n0 + kw
        m_ref[:, h:h + 1] = m_new


def _state_sample(q, k, v, so, gates, m0, n0, c0, layer, w, c_all):
    depth = c0.shape[0]
    ns = q.shape[0]
    nb = min(SEQ_BLOCK, ns)
    rows = lambda i: (i, 0)
    act_spec = pl.BlockSpec((nb, D_MODEL), rows)
    layer_slab = lambda i: (layer, i, 0, 0, 0)
    in_specs = [
        act_spec, act_spec, act_spec, act_spec,
        pl.BlockSpec((nb, LANES), rows),
        pl.BlockSpec((None, nb, H_M), lambda i: (layer, i, 0)),
        pl.BlockSpec((None, nb, H_M, DH_M), lambda i: (layer, i, 0, 0)),
        pl.BlockSpec((None, nb, H_M, DH_M, DH_M), layer_slab),
        pl.BlockSpec((None, 1, D_M), lambda i: (layer, 0, 0)),
    ]
    args = [q, k, v, so, gates, m0, n0, c0, w["mhg"]]
    aliases = {}
    if c_all is not None:
        in_specs.append(pl.BlockSpec(memory_space=pl.ANY))
        args.append(c_all)
        aliases = {len(args) - 1: 1}
    out_shape = (
        jax.ShapeDtypeStruct((ns, D_M), F32),
        jax.ShapeDtypeStruct((depth, ns, H_M, DH_M, DH_M), F32),
        jax.ShapeDtypeStruct((ns, H_M, DH_M), F32),
        jax.ShapeDtypeStruct((ns, H_M), F32),
    )
    out_specs = (
        act_spec,
        pl.BlockSpec((None, nb, H_M, DH_M, DH_M), layer_slab),
        pl.BlockSpec((nb, H_M, DH_M), lambda i: (i, 0, 0)),
        pl.BlockSpec((nb, H_M), rows),
    )
    return pl.pallas_call(
        _state_sample_kernel,
        grid=(ns // nb,),
        in_specs=in_specs,
        out_specs=out_specs,
        out_shape=out_shape,
        input_output_aliases=aliases,
        scratch_shapes=[pltpu.VMEM((nb, DH_M), F32)],
        compiler_params=pltpu.CompilerParams(
            dimension_semantics=("arbitrary",), vmem_limit_bytes=VMEM_LIMIT_BYTES),
        name="state_sample",
    )(*args)


def _out_sample_kernel(x_ref, hm_ref, part_ref, sgb_ref, older_ref, newer_ref, wpb_ref, wmix_ref,
                       ln1g_ref, ln1b_ref, wup_ref, wconv_ref, wdown_ref, ln2g_ref, ln2b_ref,
                       x2_ref, up_ref,
                       x1_ref, xb_ref, gate_ref, acc_ref):
    k = pl.program_id(0)

    @pl.when(k == 0)
    def _mix():
        yb = _dot(hm_ref[...].astype(BF16), wpb_ref[...])
        merged = part_ref[...] + sgb_ref[...] * yb
        y = ALPHA * x_ref[...] + _dot(merged.astype(BF16), wmix_ref[...])
        x1 = _layer_norm(y, ln1g_ref[...], ln1b_ref[...])
        x1_ref[...] = x1
        xb_ref[...] = x1.astype(BF16)
        acc_ref[...] = jnp.zeros(acc_ref.shape, F32)

    val = _dot(xb_ref[...], wup_ref[...])
    wc = wconv_ref[...]
    conv = older_ref[...] * wc[0:1, :] + newer_ref[...] * wc[1:2, :] + val * wc[2:3, :]
    up_ref[...] = val

    @pl.when(k < N_FF_CHUNKS)
    def _gate():
        gate_ref[k] = conv * _sigmoid(conv)

    @pl.when(k >= N_FF_CHUNKS)
    def _down():
        hdn = (gate_ref[k - N_FF_CHUNKS] * conv).astype(BF16)
        acc_ref[...] += _dot(hdn, wdown_ref[...])

    @pl.when(k == 2 * N_FF_CHUNKS - 1)
    def _finish():
        x2_ref[...] = _layer_norm(ALPHA * x1_ref[...] + acc_ref[...], ln2g_ref[...], ln2b_ref[...])


def _out_sample(x, hm, part, sgb, older, newer, layer, w):
    ns = x.shape[0]
    act = pl.BlockSpec((ns, D_MODEL), lambda k: (0, 0))
    col_chunk = lambda k: (layer, 0, k)
    in_specs = [
        act, act, act, act,
        pl.BlockSpec((None, ns, FF_CHUNK), col_chunk),
        pl.BlockSpec((None, ns, FF_CHUNK), col_chunk),
        _resident(w["wpb"].shape, layer),
        _resident(w["wmix"].shape, layer),
        _resident(w["ln1g"].shape, layer),
        _resident(w["ln1b"].shape, layer),
        pl.BlockSpec((None, D_MODEL, FF_CHUNK), col_chunk),
        pl.BlockSpec((None, CONV_W, FF_CHUNK), col_chunk),
        pl.BlockSpec((None, FF_CHUNK, D_MODEL),
                     lambda k: (layer, jnp.maximum(k - N_FF_CHUNKS, 0), 0)),
        _resident(w["ln2g"].shape, layer),
        _resident(w["ln2b"].shape, layer),
    ]
    names = ("wpb", "wmix", "ln1g", "ln1b", "wup", "wfconv", "wdown", "ln2g", "ln2b")
    return pl.pallas_call(
        _out_sample_kernel,
        grid=(2 * N_FF_CHUNKS,),
        in_specs=in_specs,
        out_specs=(act, pl.BlockSpec((ns, FF_CHUNK), lambda k: (0, k))),
        out_shape=(jax.ShapeDtypeStruct((ns, D_MODEL), F32),
                   jax.ShapeDtypeStruct((ns, 2 * D_FF), F32)),
        scratch_shapes=[
            pltpu.VMEM((ns, D_MODEL), F32),
            pltpu.VMEM((ns, D_MODEL), BF16),
            pltpu.VMEM((N_FF_CHUNKS, ns, FF_CHUNK), F32),
            pltpu.VMEM((ns, D_MODEL), F32),
        ],
        compiler_params=pltpu.CompilerParams(
            dimension_semantics=("arbitrary",), vmem_limit_bytes=VMEM_LIMIT_BYTES),
        name="out_sample",
    )(x, hm, part, sgb, older, newer, *[w[n] for n in names])


def _pack_weights(w_in, b_igate, b_fgate, w_conv_mix, mhln_g, w_proj_a, w_proj_b, w_mix_out,
                  ln1_g, ln1_b, w_ffn_up, w_ffn_conv, w_ffn_down, ln2_g, ln2_b):
    depth = w_in.shape[0]
    o_a = 3 * D_CONV
    o_gate = o_a + 4 * D_M
    o_g = o_gate + 2 * H_M
    w_t = jnp.swapaxes(w_in, 1, 2).astype(BF16)
    wgate = w_t[:, o_gate:o_g, :]
    gbias = jnp.concatenate([b_igate, b_fgate], axis=-1)
    gbias_row = jnp.pad(gbias, ((0, 0), (0, LANES - 2 * H_M)))[:, None, :]

    row = lambda a: a[:, None, :]
    return {
        "win": w_t,
        "wgt": wgate,
        "gbias": gbias[:, :, None],
        "wgc": jnp.pad(wgate, ((0, 0), (0, LANES - 2 * H_M), (0, 0))),
        "gbias_row": gbias_row,
        "wg": w_t[:, o_g:o_g + 2 * D_MODEL, :],
        "wconv": w_conv_mix,
        "mhg": row(mhln_g),
        "wpa": w_proj_a.astype(BF16),
        "wpb": w_proj_b.astype(BF16),
        "wmix": w_mix_out.astype(BF16),
        "ln1g": row(ln1_g), "ln1b": row(ln1_b),
        "wup": w_ffn_up.astype(BF16),
        "wfconv": w_ffn_conv,
        "wdown": w_ffn_down.astype(BF16),
        "ln2g": row(ln2_g), "ln2b": row(ln2_b),
    }


def kernel(x_prompt, x_sample, cache_sconv, state_mlstm_C, state_mlstm_n, state_mlstm_m, cache_ffn_conv, w_in, b_igate, b_fgate, w_conv_mix, mhln_g, w_proj_a, w_proj_b, w_mix_out, ln1_g, ln1_b, w_ffn_up, w_ffn_conv, w_ffn_down, ln2_g, ln2_b):
    depth = w_in.shape[0]
    ns = x_sample.shape[0]
    assert x_sample.shape[1] == 1
    assert x_prompt.shape[1] % CHUNK == 0
    w = _pack_weights(w_in, b_igate, b_fgate, w_conv_mix, mhln_g, w_proj_a, w_proj_b, w_mix_out,
                      ln1_g, ln1_b, w_ffn_up, w_ffn_conv, w_ffn_down, ln2_g, ln2_b)

    sconv_old, sconv_new = cache_sconv[:, :, 0, :], cache_sconv[:, :, 1, :]
    ffn_old, ffn_new = cache_ffn_conv[:, :, 0, :], cache_ffn_conv[:, :, 1, :]

    xp = x_prompt
    xs = x_sample.reshape(ns, D_MODEL)
    outs = [[] for _ in range(9)]
    c_sample = None
    for l in range(depth):
        xp, sp, cp, np_, mp = _mixer_prompt(xp, l, w)
        xp, fp = _ffn_prompt(xp, l, w)

        q, k, v, so, gates, part, sgb, ss = _proj_sample(xs, sconv_old, sconv_new, l, w)
        hm, c_sample, ns_, ms = _state_sample(q, k, v, so, gates, state_mlstm_m, state_mlstm_n,
                                              state_mlstm_C, l, w, c_sample)
        xs, fs = _out_sample(xs, hm, part, sgb, ffn_old, ffn_new, l, w)

        for lst, val in zip(outs, (sp, ss, cp, np_, ns_, mp, ms, fp, fs)):
            lst.append(val)

    sp, ss, cp, np_, ns_, mp, ms, fp, fs = (jnp.stack(o) for o in outs)
    age = lax.broadcasted_iota(jnp.int32, (1, 1, CONV_W - 1, 1), 2)
    ss = jnp.where(age == 0, sconv_new[:, :, None, :], ss[:, :, None, :])
    fs = jnp.where(age == 0, ffn_new[:, :, None, :], fs[:, :, None, :])
    return (xp, xs.reshape(ns, 1, D_MODEL), sp, ss, cp, c_sample, np_, ns_, mp, ms, fp, fs)
```

```python
import jax
import jax.numpy as jnp
from jax import lax
from jax.experimental import pallas as pl
from jax.experimental.pallas import tpu as pltpu

F32 = jnp.float32
BF16 = jnp.bfloat16

D_MODEL = 1024
D_CONV = 512
H_M = 4
DH_M = 256
D_M = H_M * DH_M
CONV_W = 3
D_FF = 2816
EPS = 1e-5
DEPTH = 4
ALPHA = (2.0 * DEPTH) ** 0.25
K_SCALE = DH_M ** -0.5
O_QKVO = 3 * D_CONV

SUBLANES = 8
LANES = 128
VMEM_LIMIT_BYTES = 56 * 1024 * 1024

MIX_CHUNK = 256
MIX_OUT_BLOCKS = 2
TILE_T = 512
FF_CHUNK = 256
N_FF_CHUNKS = D_FF // FF_CHUNK
FFN_TILE_T = 512
FFN_ROW_BLOCK = 256
SEQ_BLOCK = 8
HALO_ROW = SUBLANES - (CONV_W - 1)


def _dot(a, b):
    return jnp.dot(a, b, preferred_element_type=F32)


def _dot_nt(a, b):
    return lax.dot_general(a, b, (((1,), (1,)), ((), ())), preferred_element_type=F32)


def _sigmoid(x):
    return 1.0 / (1.0 + jnp.exp(-x))


def _log_sigmoid(x):
    return -(jnp.maximum(-x, 0.0) + jnp.log1p(jnp.exp(-jnp.abs(x))))


def _layer_norm(y, g, b):
    mu = jnp.mean(y, axis=-1, keepdims=True)
    yc = y - mu
    var = jnp.mean(yc * yc, axis=-1, keepdims=True)
    return yc * lax.rsqrt(var + EPS) * g + b


def _head_norm(h):
    mu = jnp.mean(h, axis=-1, keepdims=True)
    hc = h - mu
    var = jnp.mean(hc * hc, axis=-1, keepdims=True)
    return hc * lax.rsqrt(var + EPS)


def _resident(shape, layer):
    nd = len(shape)
    return pl.BlockSpec((None,) + tuple(shape[1:]),
                        lambda *_: (layer,) + (0,) * (nd - 1),
                        pipeline_mode=pl.Buffered(1))


def _shift_rows(val, prev8, k):
    sub = lax.broadcasted_iota(jnp.int32, prev8.shape, 0)
    rolled = pltpu.roll(val, k, 0)
    top = jnp.where(sub < k, pltpu.roll(prev8, k, 0), rolled[0:SUBLANES, :])
    return jnp.concatenate([top, rolled[SUBLANES:, :]], axis=0)


def _causal_conv3(val, prev8, wc):
    return (_shift_rows(val, prev8, 2) * wc[0:1, :] + _shift_rows(val, prev8, 1) * wc[1:2, :]
            + val * wc[2:3, :])


def _lane_scan(x, op, fill, lane_id):
    shift = 1
    while shift < x.shape[1]:
        x = op(x, jnp.where(lane_id >= shift, pltpu.roll(x, shift, 1), fill))
        shift *= 2
    return x


def _mixer_prompt_kernel(x_ref, win_ref, wgt_ref, gbias_ref, wg_ref, wconv_ref,
                         mhg_ref, wpa_ref, wpb_ref, wmix_ref, lng_ref, lnb_ref,
                         x1_ref, sconv_ref, c_ref, n_ref, m_ref,
                         xb_ref, pbuf_ref, merged_ref, q_ref, kt_ref, va_ref, o_ref, hm_ref,
                         caug_ref, mlane_ref, r_ref, w_ref, scol_ref, colmx_ref, colsc_ref,
                         coleinv_ref):
    t = pl.program_id(1)
    tile = x_ref.shape[0]
    L = MIX_CHUNK
    n_chunks = tile // L

    @pl.when(t == 0)
    def _init():
        pbuf_ref[...] = jnp.zeros((SUBLANES, D_CONV), F32)
        caug_ref[...] = jnp.zeros(caug_ref.shape, F32)
        m_ref[...] = jnp.zeros(m_ref.shape, F32)
        mlane_ref[...] = jnp.zeros(mlane_ref.shape, F32)

    x = x_ref[...]
    xb_ref[...] = x.astype(BF16)
    xb = xb_ref[...]

    za = _dot_nt(xb, win_ref[0:O_QKVO, :])
    bg = za[:, 0:D_CONV]
    p = za[:, D_CONV:2 * D_CONV] * za[:, 2 * D_CONV:3 * D_CONV]
    conv = _causal_conv3(p, pbuf_ref[...], wconv_ref[...])
    halo = p[tile - SUBLANES:tile, :]
    pbuf_ref[...] = halo
    sconv_ref[...] = halo
    gated = (bg * conv).astype(BF16)

    kg = _dot_nt(wgt_ref[...], xb)
    kt_ref[0] = kg[0:DH_M, :] * K_SCALE
    gates = kg[DH_M:DH_M + 2 * H_M, :] + gbias_ref[...]
    va_ref[:, :, DH_M:DH_M + LANES] = jnp.ones((H_M, tile, LANES), BF16)

    def project_head(h):
        hc_ = slice(h * DH_M, (h + 1) * DH_M)
        w_rows = lambda i: win_ref[O_QKVO + i * D_M + h * DH_M:
                                   O_QKVO + i * D_M + (h + 1) * DH_M, :]
        q_ref[:, hc_] = _dot_nt(xb_ref[...], w_rows(0)).astype(BF16)
        if h > 0:
            kt_ref[h] = _dot_nt(w_rows(1), xb_ref[...]) * K_SCALE
        va_ref[h, :, 0:DH_M] = _dot_nt(xb_ref[...], w_rows(2)).astype(BF16)
        o_ref[:, hc_] = _sigmoid(_dot_nt(xb_ref[...], w_rows(3))) * mhg_ref[:, hc_]

    project_head(0)
    project_head(1)
    ya = _dot(gated, wpa_ref[...])
    ga = _dot_nt(xb_ref[...], wg_ref[0:D_MODEL, :])
    project_head(2)
    project_head(3)
    merged_ref[...] = _sigmoid(ga) * ya

    row_id = lax.broadcasted_iota(jnp.int32, (2 * H_M, L), 0)
    lane_id = lax.broadcasted_iota(jnp.int32, (2 * H_M, L), 1)
    head_rows = lax.broadcasted_iota(jnp.int32, (2 * H_M, 1), 0) < H_M
    m8 = m_ref[:, 0:1]
    mlane = mlane_ref[0:1, :]
    decays = []
    for c in range(n_chunks):
        g = gates[:, c * L:(c + 1) * L]
        csum = _lane_scan(_log_sigmoid(g), jnp.add, 0.0, lane_id)
        b4 = pltpu.roll(csum, H_M, 0)
        r = g - b4
        gmax = _lane_scan(r, jnp.maximum, -jnp.inf, lane_id)
        comb = jnp.where(row_id < H_M, gmax, csum)
        scol = jnp.concatenate([comb, jnp.zeros((LANES - 2 * H_M, L), F32)], axis=0).T
        scol_ref[c * L:(c + 1) * L, :] = scol

        mx_last = jnp.maximum(m8, gmax[:, L - 1:L])
        decays.append(jnp.exp(m8 - mx_last))
        r_ref[c] = r
        w_ref[c] = jnp.exp(r - mx_last)
        m8 = jnp.where(head_rows, b4[:, L - 1:L] + mx_last, 0.0)

        sc = scol_ref[c * L:(c + 1) * L, :]
        mxc = jnp.maximum(mlane, sc)
        colmx_ref[c * L:(c + 1) * L, :] = mxc
        colsc_ref[c * L:(c + 1) * L, :] = jnp.exp(mlane - mxc)
        coleinv_ref[c * L:(c + 1) * L, :] = jnp.exp(-(pltpu.roll(sc, LANES - H_M, 1) + mxc))
        last = sc[L - 1:L, :]
        mlane = pltpu.roll(last, LANES - H_M, 1) + jnp.maximum(mlane, last)
    m_ref[...] = jnp.broadcast_to(m8, m_ref.shape)
    mlane_ref[...] = jnp.broadcast_to(mlane, mlane_ref.shape)

    tri = (lax.broadcasted_iota(jnp.int32, (L, L), 0)
           >= lax.broadcasted_iota(jnp.int32, (L, L), 1))

    for c in range(n_chunks):
        rows = slice(c * L, (c + 1) * L)
        for h in range(H_M):
            hc_ = slice(h * DH_M, (h + 1) * DH_M)
            q = q_ref[rows, hc_]
            kt = kt_ref[h, :, rows]
            va = va_ref[h, rows, :]
            r_row = r_ref[c, h:h + 1, :]
            w_row = w_ref[c, h:h + 1, :]
            mx_col = colmx_ref[rows, h:h + 1]
            sc_col = colsc_ref[rows, h:h + 1]
            einv_col = coleinv_ref[rows, h:h + 1]
            decay = decays[c][h:h + 1, :]

            pmat = jnp.where(tri, jnp.exp(r_row - mx_col), 0.0)
            s = (_dot(q, kt.astype(BF16)) * pmat).astype(BF16)
            ca = caug_ref[h]
            tot = _dot(s, va) + sc_col * _dot(q, ca.astype(BF16))
            dd = jnp.maximum(jnp.abs(tot[:, DH_M:DH_M + LANES]), einv_col)
            hcell = tot[:, 0:DH_M] / jnp.concatenate([dd, dd], axis=1)
            hm = o_ref[rows, hc_] * _head_norm(hcell)
            hm_ref[rows, hc_] = hm.astype(BF16)

            kw = (kt * w_row).astype(BF16)
            caug_ref[h] = decay * ca + _dot(kw, va)

    @pl.when(t == pl.num_programs(1) - 1)
    def _emit_state():
        c_ref[...] = caug_ref[:, :, 0:DH_M]
        n_ref[...] = caug_ref[:, :, DH_M:DH_M + LANES]

    rb = tile // MIX_OUT_BLOCKS
    for r in range(MIX_OUT_BLOCKS):
        rows = slice(r * rb, (r + 1) * rb)
        gb = _dot_nt(xb_ref[rows, :], wg_ref[D_MODEL:2 * D_MODEL, :])
        yb = _dot(hm_ref[rows, :], wpb_ref[...])
        merged = merged_ref[rows, :] + _sigmoid(gb) * yb
        y = ALPHA * x_ref[rows, :] + _dot(merged.astype(BF16), wmix_ref[...])
        x1_ref[rows, :] = _layer_norm(y, lng_ref[...], lnb_ref[...])


def _mixer_prompt(x, layer, w):
    bsz, seq, _ = x.shape
    tile = min(TILE_T, seq)
    n_tiles = seq // tile
    n_chunks = tile // MIX_CHUNK
    row_block = lambda b, t: (b, t, 0)
    per_seq = lambda b, t: (b, 0, 0)
    per_seq4 = lambda b, t: (b, 0, 0, 0)
    names = ("win", "wgt", "gbias", "wg", "wconv", "mhg", "wpa", "wpb", "wmix",
             "ln1g", "ln1b")
    in_specs = [pl.BlockSpec((None, tile, D_MODEL), row_block)]
    in_specs += [_resident(w[n].shape, layer) for n in names]
    out_shape = (
        jax.ShapeDtypeStruct((bsz, seq, D_MODEL), F32),
        jax.ShapeDtypeStruct((bsz, SUBLANES, D_CONV), F32),
        jax.ShapeDtypeStruct((bsz, H_M, DH_M, DH_M), F32),
        jax.ShapeDtypeStruct((bsz, H_M, DH_M, LANES), F32),
        jax.ShapeDtypeStruct((bsz, SUBLANES, LANES), F32),
    )
    out_specs = (
        pl.BlockSpec((None, tile, D_MODEL), row_block),
        pl.BlockSpec((None, SUBLANES, D_CONV), per_seq),
        pl.BlockSpec((None, H_M, DH_M, DH_M), per_seq4),
        pl.BlockSpec((None, H_M, DH_M, LANES), per_seq4),
        pl.BlockSpec((None, SUBLANES, LANES), per_seq),
    )
    col = pltpu.VMEM((tile, LANES), F32)
    scratch = [
        pltpu.VMEM((tile, D_MODEL), BF16),
        pltpu.VMEM((SUBLANES, D_CONV), F32),
        pltpu.VMEM((tile, D_MODEL), F32),
        pltpu.VMEM((tile, D_M), BF16),
        pltpu.VMEM((H_M, DH_M, tile), F32),
        pltpu.VMEM((H_M, tile, DH_M + LANES), BF16),
        pltpu.VMEM((tile, D_M), F32),
        pltpu.VMEM((tile, D_M), BF16),
        pltpu.VMEM((H_M, DH_M, DH_M + LANES), F32),
        pltpu.VMEM((SUBLANES, LANES), F32),
        pltpu.VMEM((n_chunks, 2 * H_M, MIX_CHUNK), F32),
        pltpu.VMEM((n_chunks, 2 * H_M, MIX_CHUNK), F32),
        col, col, col, col,
    ]
    x1, sconv, c, n, m = pl.pallas_call(
        _mixer_prompt_kernel,
        grid=(bsz, n_tiles),
        in_specs=in_specs,
        out_specs=out_specs,
        out_shape=out_shape,
        scratch_shapes=scratch,
        compiler_params=pltpu.CompilerParams(
            dimension_semantics=("arbitrary", "arbitrary"),
            vmem_limit_bytes=VMEM_LIMIT_BYTES),
        name="mixer_prompt",
    )(x, *[w[n] for n in names])
    return x1, sconv[:, HALO_ROW:, :], c, n[:, :, :, 0], m[:, :H_M, 0]


def _ffn_prompt_kernel(x_ref, wup_ref, wconv_ref, wdown_ref, lng_ref, lnb_ref,
                       x2_ref, halo_ref,
                       xb_ref, acc_ref):
    t = pl.program_id(1)
    tile = x_ref.shape[0]

    @pl.when(t == 0)
    def _init():
        halo_ref[...] = jnp.zeros(halo_ref.shape, F32)

    xb_ref[...] = x_ref[...].astype(BF16)

    rb = min(FFN_ROW_BLOCK, tile)
    n_row_blocks = tile // rb
    chunk_cols = lambda j: (slice(j * FF_CHUNK, (j + 1) * FF_CHUNK),
                            slice(D_FF + j * FF_CHUNK, D_FF + (j + 1) * FF_CHUNK))

    def up_project(j, r):
        xr = xb_ref[r * rb:(r + 1) * rb, :]
        return tuple(_dot(xr, wup_ref[:, cols]) for cols in chunk_cols(j))

    pending = {(0, r): up_project(0, r) for r in range(n_row_blocks)}
    for j in range(N_FF_CHUNKS):
        prev = [halo_ref[:, cols] for cols in chunk_cols(j)]
        for r in range(n_row_blocks):
            rows = slice(r * rb, (r + 1) * rb)
            vals = pending.pop((j, r))
            g, u = (_causal_conv3(v, p8, wconv_ref[:, cols])
                    for v, p8, cols in zip(vals, prev, chunk_cols(j)))
            prev = [v[rb - SUBLANES:rb, :] for v in vals]
            hdn = (g * _sigmoid(g) * u).astype(BF16)
            if j + 1 < N_FF_CHUNKS:
                pending[(j + 1, r)] = up_project(j + 1, r)
            part = _dot(hdn, wdown_ref[j * FF_CHUNK:(j + 1) * FF_CHUNK, :])
            if j == 0:
                acc_ref[rows, :] = part
            else:
                acc_ref[rows, :] += part
        for cols, tail in zip(chunk_cols(j), prev):
            halo_ref[:, cols] = tail

    y = ALPHA * x_ref[...] + acc_ref[...]
    x2_ref[...] = _layer_norm(y, lng_ref[...], lnb_ref[...])


def _ffn_prompt(x, layer, w):
    bsz, seq, _ = x.shape
    tile = min(FFN_TILE_T, seq)
    n_tiles = seq // tile
    row_block = lambda b, t: (b, t, 0)
    in_specs = [
        pl.BlockSpec((None, tile, D_MODEL), row_block),
        _resident(w["wup"].shape, layer),
        _resident(w["wfconv"].shape, layer),
        _resident(w["wdown"].shape, layer),
        _resident(w["ln2g"].shape, layer),
        _resident(w["ln2b"].shape, layer),
    ]
    x2, halo = pl.pallas_call(
        _ffn_prompt_kernel,
        grid=(bsz, n_tiles),
        in_specs=in_specs,
        out_specs=(pl.BlockSpec((None, tile, D_MODEL), row_block),
                   pl.BlockSpec((None, SUBLANES, 2 * D_FF), lambda b, t: (b, 0, 0))),
        out_shape=(jax.ShapeDtypeStruct((bsz, seq, D_MODEL), F32),
                   jax.ShapeDtypeStruct((bsz, SUBLANES, 2 * D_FF), F32)),
        scratch_shapes=[
            pltpu.VMEM((tile, D_MODEL), BF16),
            pltpu.VMEM((tile, D_MODEL), F32),
        ],
        compiler_params=pltpu.CompilerParams(
            dimension_semantics=("arbitrary", "arbitrary"),
            vmem_limit_bytes=VMEM_LIMIT_BYTES),
        name="ffn_prompt",
    )(x, w["wup"], w["wfconv"], w["wdown"], w["ln2g"], w["ln2b"])
    return x2, halo[:, HALO_ROW:, :]


def _proj_sample_kernel(x_ref, older_ref, newer_ref, win_ref, wgc_ref, gbias_ref, wg_ref,
                        wconv_ref, wpa_ref,
                        q_ref, k_ref, v_ref, so_ref, gates_ref, part_ref, sgb_ref, p_ref):
    xb = x_ref[...].astype(BF16)
    za = _dot_nt(xb, win_ref[0:O_QKVO, :])
    bg = za[:, 0:D_CONV]
    p = za[:, D_CONV:2 * D_CONV] * za[:, 2 * D_CONV:3 * D_CONV]
    wc = wconv_ref[...]
    conv = older_ref[...] * wc[0:1, :] + newer_ref[...] * wc[1:2, :] + p * wc[2:3, :]
    p_ref[...] = p
    ya = _dot((bg * conv).astype(BF16), wpa_ref[...])
    zg = _dot_nt(xb, wg_ref[...])
    part_ref[...] = _sigmoid(zg[:, 0:D_MODEL]) * ya
    sgb_ref[...] = _sigmoid(zg[:, D_MODEL:2 * D_MODEL])
    gz = _dot_nt(xb, wgc_ref[...]) + gbias_ref[...]
    lane = lax.broadcasted_iota(jnp.int32, gz.shape, 1)
    gates_ref[...] = jnp.where(lane < H_M, gz, _log_sigmoid(gz))
    w_rows = lambda i: win_ref[O_QKVO + i * D_M:O_QKVO + (i + 1) * D_M, :]
    q_ref[...] = _dot_nt(xb, w_rows(0))
    k_ref[...] = _dot_nt(xb, w_rows(1)) * K_SCALE
    v_ref[...] = _dot_nt(xb, w_rows(2))
    so_ref[...] = _sigmoid(_dot_nt(xb, w_rows(3)))


def _proj_sample(x, older, newer, layer, w):
    ns = x.shape[0]
    whole = lambda shape: pl.BlockSpec(shape, lambda i: (0,) * len(shape))
    act = jax.ShapeDtypeStruct((ns, D_MODEL), F32)
    in_specs = [
        whole((ns, D_MODEL)),
        pl.BlockSpec((None, ns, D_CONV), lambda i: (layer, 0, 0)),
        pl.BlockSpec((None, ns, D_CONV), lambda i: (layer, 0, 0)),
        _resident(w["win"].shape, layer),
        _resident(w["wgc"].shape, layer),
        _resident(w["gbias_row"].shape, layer),
        _resident(w["wg"].shape, layer),
        _resident(w["wconv"].shape, layer),
        _resident(w["wpa"].shape, layer),
    ]
    out_shape = (act, act, act, act, jax.ShapeDtypeStruct((ns, LANES), F32), act, act,
                 jax.ShapeDtypeStruct((ns, D_CONV), F32))
    out_specs = tuple(whole(s.shape) for s in out_shape)
    return pl.pallas_call(
        _proj_sample_kernel,
        grid=(1,),
        in_specs=in_specs,
        out_specs=out_specs,
        out_shape=out_shape,
        compiler_params=pltpu.CompilerParams(
            dimension_semantics=("arbitrary",), vmem_limit_bytes=VMEM_LIMIT_BYTES),
        name="proj_sample",
    )(x, older, newer, w["win"], w["wgc"], w["gbias_row"], w["wg"], w["wconv"], w["wpa"])


def _state_sample_kernel(q_ref, k_ref, v_ref, so_ref, gates_ref, m0_ref, n0_ref, c0_ref, mhg_ref,
                         *rest):
    hm_ref, c_ref, n_ref, m_ref, qc_ref = rest[-5:]
    nb = q_ref.shape[0]
    gates = gates_ref[...]
    zpad = jnp.zeros((LANES - nb, DH_M), F32)
    for h in range(H_M):
        cols = slice(h * DH_M, (h + 1) * DH_M)
        q = q_ref[:, cols]
        k = k_ref[:, cols]
        v = v_ref[:, cols]
        li = gates[:, h:h + 1]
        lf = gates[:, H_M + h:H_M + h + 1]
        m0 = m0_ref[:, h:h + 1]
        n0 = n0_ref[:, h, :]
        m_new = jnp.maximum(lf + m0, li)
        decay = jnp.exp(lf + m0 - m_new)
        wgt = jnp.exp(li - m_new)
        kw = k * wgt
        qb = q.astype(BF16)
        kw_t = jnp.concatenate([kw, zpad], axis=0).T
        v_pad = jnp.concatenate([v, zpad], axis=0).astype(BF16)
        seq_lane = lax.broadcasted_iota(jnp.int32, kw_t.shape, 1)
        for i in range(nb):
            c0 = c0_ref[i, h]
            qc_ref[i:i + 1, :] = _dot(qb, c0.astype(BF16))[i:i + 1, :]
            kw_i = jnp.where(seq_lane == i, kw_t, 0.0).astype(BF16)
            c_ref[i, h] = decay[i:i + 1, :] * c0 + _dot(kw_i, v_pad)
        s = jnp.sum(q * k, axis=-1, keepdims=True) * wgt
        numer = s * v + decay * qc_ref[...]
        den = s + decay * jnp.sum(q * n0, axis=-1, keepdims=True)
        hc = numer / jnp.maximum(jnp.abs(den), jnp.exp(-m_new))
        hm_ref[:, cols] = so_ref[:, cols] * (_head_norm(hc) * mhg_ref[:, cols])
        n_ref[:, h, :] = decay * n0 + kw
        m_ref[:, h:h + 1] = m_new


def _state_sample(q, k, v, so, gates, m0, n0, c0, layer, w, c_all):
    depth = c0.shape[0]
    ns = q.shape[0]
    nb = min(SEQ_BLOCK, ns)
    rows = lambda i: (i, 0)
    act_spec = pl.BlockSpec((nb, D_MODEL), rows)
    layer_slab = lambda i: (layer, i, 0, 0, 0)
    in_specs = [
        act_spec, act_spec, act_spec, act_spec,
        pl.BlockSpec((nb, LANES), rows),
        pl.BlockSpec((None, nb, H_M), lambda i: (layer, i, 0)),
        pl.BlockSpec((None, nb, H_M, DH_M), lambda i: (layer, i, 0, 0)),
        pl.BlockSpec((None, nb, H_M, DH_M, DH_M), layer_slab),
        pl.BlockSpec((None, 1, D_M), lambda i: (layer, 0, 0)),
    ]
    args = [q, k, v, so, gates, m0, n0, c0, w["mhg"]]
    aliases = {}
    if c_all is not None:
        in_specs.append(pl.BlockSpec(memory_space=pl.ANY))
        args.append(c_all)
        aliases = {len(args) - 1: 1}
    out_shape = (
        jax.ShapeDtypeStruct((ns, D_M), F32),
        jax.ShapeDtypeStruct((depth, ns, H_M, DH_M, DH_M), F32),
        jax.ShapeDtypeStruct((ns, H_M, DH_M), F32),
        jax.ShapeDtypeStruct((ns, H_M), F32),
    )
    out_specs = (
        act_spec,
        pl.BlockSpec((None, nb, H_M, DH_M, DH_M), layer_slab),
        pl.BlockSpec((nb, H_M, DH_M), lambda i: (i, 0, 0)),
        pl.BlockSpec((nb, H_M), rows),
    )
    return pl.pallas_call(
        _state_sample_kernel,
        grid=(ns // nb,),
        in_specs=in_specs,
        out_specs=out_specs,
        out_shape=out_shape,
        input_output_aliases=aliases,
        scratch_shapes=[pltpu.VMEM((nb, DH_M), F32)],
        compiler_params=pltpu.CompilerParams(
            dimension_semantics=("arbitrary",), vmem_limit_bytes=VMEM_LIMIT_BYTES),
        name="state_sample",
    )(*args)


def _out_sample_kernel(x_ref, hm_ref, part_ref, sgb_ref, older_g, older_u, newer_g, newer_u,
                       wpb_ref, wmix_ref, ln1g_ref, ln1b_ref, wup_g, wup_u, wconv_g, wconv_u,
                       wdown_ref, ln2g_ref, ln2b_ref,
                       x2_ref, upg_ref, upu_ref,
                       x1_ref, xb_ref, acc_ref):
    k = pl.program_id(0)

    @pl.when(k == 0)
    def _mix():
        yb = _dot(hm_ref[...].astype(BF16), wpb_ref[...])
        merged = part_ref[...] + sgb_ref[...] * yb
        y = ALPHA * x_ref[...] + _dot(merged.astype(BF16), wmix_ref[...])
        x1 = _layer_norm(y, ln1g_ref[...], ln1b_ref[...])
        x1_ref[...] = x1
        xb_ref[...] = x1.astype(BF16)
        acc_ref[...] = jnp.zeros(acc_ref.shape, F32)

    def conv(older, newer, wup, wconv, up_out):
        val = _dot(xb_ref[...], wup[...])
        up_out[...] = val
        wc = wconv[...]
        return older[...] * wc[0:1, :] + newer[...] * wc[1:2, :] + val * wc[2:3, :]

    g = conv(older_g, newer_g, wup_g, wconv_g, upg_ref)
    u = conv(older_u, newer_u, wup_u, wconv_u, upu_ref)
    acc_ref[...] += _dot((g * _sigmoid(g) * u).astype(BF16), wdown_ref[...])

    @pl.when(k == N_FF_CHUNKS - 1)
    def _finish():
        x2_ref[...] = _layer_norm(ALPHA * x1_ref[...] + acc_ref[...], ln2g_ref[...], ln2b_ref[...])


def _out_sample(x, hm, part, sgb, older, newer, layer, w):
    ns = x.shape[0]
    act = pl.BlockSpec((ns, D_MODEL), lambda k: (0, 0))
    gate_cols = lambda k: (layer, 0, k)
    value_cols = lambda k: (layer, 0, N_FF_CHUNKS + k)
    both = lambda shape: [pl.BlockSpec(shape, gate_cols), pl.BlockSpec(shape, value_cols)]
    in_specs = (
        [act, act, act, act]
        + both((None, ns, FF_CHUNK)) + both((None, ns, FF_CHUNK))
        + [_resident(w[n].shape, layer) for n in ("wpb", "wmix", "ln1g", "ln1b")]
        + both((None, D_MODEL, FF_CHUNK)) + both((None, CONV_W, FF_CHUNK))
        + [pl.BlockSpec((None, FF_CHUNK, D_MODEL), lambda k: (layer, k, 0)),
           _resident(w["ln2g"].shape, layer), _resident(w["ln2b"].shape, layer)])
    up_half = jax.ShapeDtypeStruct((ns, D_FF), F32)
    up_spec = pl.BlockSpec((ns, FF_CHUNK), lambda k: (0, k))
    x2, up_g, up_u = pl.pallas_call(
        _out_sample_kernel,
        grid=(N_FF_CHUNKS,),
        in_specs=in_specs,
        out_specs=(act, up_spec, up_spec),
        out_shape=(jax.ShapeDtypeStruct((ns, D_MODEL), F32), up_half, up_half),
        scratch_shapes=[
            pltpu.VMEM((ns, D_MODEL), F32),
            pltpu.VMEM((ns, D_MODEL), BF16),
            pltpu.VMEM((ns, D_MODEL), F32),
        ],
        compiler_params=pltpu.CompilerParams(
            dimension_semantics=("arbitrary",), vmem_limit_bytes=VMEM_LIMIT_BYTES),
        name="out_sample",
    )(x, hm, part, sgb, older, older, newer, newer, w["wpb"], w["wmix"], w["ln1g"], w["ln1b"],
      w["wup"], w["wup"], w["wfconv"], w["wfconv"], w["wdown"], w["ln2g"], w["ln2b"])
    return x2, jnp.concatenate([up_g, up_u], axis=-1)


def _pack_weights(w_in, b_igate, b_fgate, w_conv_mix, mhln_g, w_proj_a, w_proj_b, w_mix_out,
                  ln1_g, ln1_b, w_ffn_up, w_ffn_conv, w_ffn_down, ln2_g, ln2_b):
    depth = w_in.shape[0]
    o_a = 3 * D_CONV
    o_gate = o_a + 4 * D_M
    o_g = o_gate + 2 * H_M
    w_t = jnp.swapaxes(w_in, 1, 2).astype(BF16)
    wgate = w_t[:, o_gate:o_g, :]
    gbias = jnp.concatenate([b_igate, b_fgate], axis=-1)
    gbias_row = jnp.pad(gbias, ((0, 0), (0, LANES - 2 * H_M)))[:, None, :]

    row = lambda a: a[:, None, :]
    return {
        "win": w_t,
        "wgt": jnp.concatenate([w_t[:, o_a + D_M:o_a + D_M + DH_M, :],
                                jnp.pad(wgate, ((0, 0), (0, 2 * H_M), (0, 0)))], axis=1),
        "gbias": gbias[:, :, None],
        "wgc": jnp.pad(wgate, ((0, 0), (0, LANES - 2 * H_M), (0, 0))),
        "gbias_row": gbias_row,
        "wg": w_t[:, o_g:o_g + 2 * D_MODEL, :],
        "wconv": w_conv_mix,
        "mhg": row(mhln_g),
        "wpa": w_proj_a.astype(BF16),
        "wpb": w_proj_b.astype(BF16),
        "wmix": w_mix_out.astype(BF16),
        "ln1g": row(ln1_g), "ln1b": row(ln1_b),
        "wup": w_ffn_up.astype(BF16),
        "wfconv": w_ffn_conv,
        "wdown": w_ffn_down.astype(BF16),
        "ln2g": row(ln2_g), "ln2b": row(ln2_b),
    }


def kernel(x_prompt, x_sample, cache_sconv, state_mlstm_C, state_mlstm_n, state_mlstm_m, cache_ffn_conv, w_in, b_igate, b_fgate, w_conv_mix, mhln_g, w_proj_a, w_proj_b, w_mix_out, ln1_g, ln1_b, w_ffn_up, w_ffn_conv, w_ffn_down, ln2_g, ln2_b):
    depth = w_in.shape[0]
    ns = x_sample.shape[0]
    seq = x_prompt.shape[1]
    assert x_sample.shape[1] == 1
    assert seq % min(TILE_T, seq) == 0 and min(TILE_T, seq) % MIX_CHUNK == 0
    assert seq % min(FFN_TILE_T, seq) == 0 and min(FFN_TILE_T, seq) % FFN_ROW_BLOCK == 0
    assert ns % min(SEQ_BLOCK, ns) == 0
    w = _pack_weights(w_in, b_igate, b_fgate, w_conv_mix, mhln_g, w_proj_a, w_proj_b, w_mix_out,
                      ln1_g, ln1_b, w_ffn_up, w_ffn_conv, w_ffn_down, ln2_g, ln2_b)

    sconv_old, sconv_new = cache_sconv[:, :, 0, :], cache_sconv[:, :, 1, :]
    ffn_old, ffn_new = cache_ffn_conv[:, :, 0, :], cache_ffn_conv[:, :, 1, :]

    xp = x_prompt
    xs = x_sample.reshape(ns, D_MODEL)
    outs = [[] for _ in range(9)]
    c_sample = None
    for l in range(depth):
        xp, sp, cp, np_, mp = _mixer_prompt(xp, l, w)
        xp, fp = _ffn_prompt(xp, l, w)

        q, k, v, so, gates, part, sgb, ss = _proj_sample(xs, sconv_old, sconv_new, l, w)
        hm, c_sample, ns_, ms = _state_sample(q, k, v, so, gates, state_mlstm_m, state_mlstm_n,
                                              state_mlstm_C, l, w, c_sample)
        xs, fs = _out_sample(xs, hm, part, sgb, ffn_old, ffn_new, l, w)

        for lst, val in zip(outs, (sp, ss, cp, np_, ns_, mp, ms, fp, fs)):
            lst.append(val)

    sp, ss, cp, np_, ns_, mp, ms, fp, fs = (jnp.stack(o) for o in outs)
    age = lax.broadcasted_iota(jnp.int32, (1, 1, CONV_W - 1, 1), 2)
    ss = jnp.where(age == 0, sconv_new[:, :, None, :], ss[:, :, None, :])
    fs = jnp.where(age == 0, ffn_new[:, :, None, :], fs[:, :, None, :])
    return (xp, xs.reshape(ns, 1, D_MODEL), sp, ss, cp, c_sample, np_, ns_, mp, ms, fp, fs)
```

```python
import jax
import jax.numpy as jnp
from jax import lax
from jax.experimental import pallas as pl
from jax.experimental.pallas import tpu as pltpu

F32 = jnp.float32
BF16 = jnp.bfloat16

D_MODEL = 1024
D_CONV = 512
H_M = 4
DH_M = 256
D_M = H_M * DH_M
CONV_W = 3
D_FF = 2816
EPS = 1e-5
DEPTH = 4
ALPHA = (2.0 * DEPTH) ** 0.25
K_SCALE = DH_M ** -0.5
O_QKVO = 3 * D_CONV

SUBLANES = 8
LANES = 128
VMEM_LIMIT_BYTES = 56 * 1024 * 1024

MIX_CHUNK = 256
MIX_OUT_BLOCKS = 2
TILE_T = 512
FF_CHUNK = 256
N_FF_CHUNKS = D_FF // FF_CHUNK
FFN_TILE_T = 512
FFN_ROW_BLOCK = 256
SEQ_BLOCK = 8
HALO_ROW = SUBLANES - (CONV_W - 1)


def _dot(a, b):
    return jnp.dot(a, b, preferred_element_type=F32)


def _dot_nt(a, b):
    return lax.dot_general(a, b, (((1,), (1,)), ((), ())), preferred_element_type=F32)


def _sigmoid(x):
    return 1.0 / (1.0 + jnp.exp(-x))


def _log_sigmoid(x):
    return -(jnp.maximum(-x, 0.0) + jnp.log1p(jnp.exp(-jnp.abs(x))))


def _layer_norm(y, g, b):
    mu = jnp.mean(y, axis=-1, keepdims=True)
    yc = y - mu
    var = jnp.mean(yc * yc, axis=-1, keepdims=True)
    return yc * lax.rsqrt(var + EPS) * g + b


def _head_norm(h):
    mu = jnp.mean(h, axis=-1, keepdims=True)
    hc = h - mu
    var = jnp.mean(hc * hc, axis=-1, keepdims=True)
    return hc * lax.rsqrt(var + EPS)


def _resident(shape, layer):
    nd = len(shape)
    return pl.BlockSpec((None,) + tuple(shape[1:]),
                        lambda *_: (layer,) + (0,) * (nd - 1),
                        pipeline_mode=pl.Buffered(1))


def _shift_rows(val, prev8, k):
    sub = lax.broadcasted_iota(jnp.int32, prev8.shape, 0)
    rolled = pltpu.roll(val, k, 0)
    top = jnp.where(sub < k, pltpu.roll(prev8, k, 0), rolled[0:SUBLANES, :])
    return jnp.concatenate([top, rolled[SUBLANES:, :]], axis=0)


def _causal_conv3(val, prev8, wc):
    return (_shift_rows(val, prev8, 2) * wc[0:1, :] + _shift_rows(val, prev8, 1) * wc[1:2, :]
            + val * wc[2:3, :])


def _lane_scan(x, op, fill, lane_id):
    shift = 1
    while shift < x.shape[1]:
        x = op(x, jnp.where(lane_id >= shift, pltpu.roll(x, shift, 1), fill))
        shift *= 2
    return x


def _mixer_prompt_kernel(x_ref, win_ref, wgt_ref, gbias_ref, wg_ref, wconv_ref,
                         mhg_ref, wpa_ref, wpb_ref, wmix_ref, lng_ref, lnb_ref,
                         x1_ref, sconv_ref, c_ref, n_ref, m_ref,
                         xb_ref, pbuf_ref, merged_ref, q_ref, kt_ref, va_ref, o_ref, hm_ref,
                         caug_ref, mlane_ref, r_ref, w_ref, scol_ref, colmx_ref, colsc_ref,
                         coleinv_ref, gb_ref):
    t = pl.program_id(1)
    tile = x_ref.shape[0]
    L = MIX_CHUNK
    n_chunks = tile // L

    @pl.when(t == 0)
    def _init():
        pbuf_ref[...] = jnp.zeros((SUBLANES, D_CONV), F32)
        caug_ref[...] = jnp.zeros(caug_ref.shape, F32)
        m_ref[...] = jnp.zeros(m_ref.shape, F32)
        mlane_ref[...] = jnp.zeros(mlane_ref.shape, F32)

    x = x_ref[...]
    xb_ref[...] = x.astype(BF16)
    xb = xb_ref[...]

    za = _dot_nt(xb, win_ref[0:O_QKVO, :])
    bg = za[:, 0:D_CONV]
    p = za[:, D_CONV:2 * D_CONV] * za[:, 2 * D_CONV:3 * D_CONV]
    conv = _causal_conv3(p, pbuf_ref[...], wconv_ref[...])
    halo = p[tile - SUBLANES:tile, :]
    pbuf_ref[...] = halo
    sconv_ref[...] = halo
    gated = (bg * conv).astype(BF16)

    kg = _dot_nt(wgt_ref[...], xb)
    kt_ref[0] = kg[0:DH_M, :] * K_SCALE
    gates = kg[DH_M:DH_M + 2 * H_M, :] + gbias_ref[...]
    va_ref[:, :, DH_M:DH_M + LANES] = jnp.ones((H_M, tile, LANES), BF16)

    def project_head(h):
        hc_ = slice(h * DH_M, (h + 1) * DH_M)
        w_rows = lambda i: win_ref[O_QKVO + i * D_M + h * DH_M:
                                   O_QKVO + i * D_M + (h + 1) * DH_M, :]
        q_ref[:, hc_] = _dot_nt(xb_ref[...], w_rows(0)).astype(BF16)
        if h > 0:
            kt_ref[h] = _dot_nt(w_rows(1), xb_ref[...]) * K_SCALE
        va_ref[h, :, 0:DH_M] = _dot_nt(xb_ref[...], w_rows(2)).astype(BF16)
        o_ref[:, hc_] = _sigmoid(_dot_nt(xb_ref[...], w_rows(3))) * mhg_ref[:, hc_]

    project_head(0)
    project_head(1)
    ya = _dot(gated, wpa_ref[...])
    ga = _dot_nt(xb_ref[...], wg_ref[0:D_MODEL, :])
    project_head(2)
    project_head(3)
    merged_ref[...] = _sigmoid(ga) * ya

    row_id = lax.broadcasted_iota(jnp.int32, (2 * H_M, L), 0)
    lane_id = lax.broadcasted_iota(jnp.int32, (2 * H_M, L), 1)
    head_rows = lax.broadcasted_iota(jnp.int32, (2 * H_M, 1), 0) < H_M
    m8 = m_ref[:, 0:1]
    mlane = mlane_ref[0:1, :]
    decays = []
    for c in range(n_chunks):
        g = gates[:, c * L:(c + 1) * L]
        csum = _lane_scan(_log_sigmoid(g), jnp.add, 0.0, lane_id)
        b4 = pltpu.roll(csum, H_M, 0)
        r = g - b4
        gmax = _lane_scan(r, jnp.maximum, -jnp.inf, lane_id)
        comb = jnp.where(row_id < H_M, gmax, csum)
        scol = jnp.concatenate([comb, jnp.zeros((LANES - 2 * H_M, L), F32)], axis=0).T
        scol_ref[c * L:(c + 1) * L, :] = scol

        mx_last = jnp.maximum(m8, gmax[:, L - 1:L])
        decays.append(jnp.exp(m8 - mx_last))
        r_ref[c] = r
        w_ref[c] = jnp.exp(r - mx_last)
        m8 = jnp.where(head_rows, b4[:, L - 1:L] + mx_last, 0.0)

        sc = scol_ref[c * L:(c + 1) * L, :]
        mxc = jnp.maximum(mlane, sc)
        colmx_ref[c * L:(c + 1) * L, :] = mxc
        colsc_ref[c * L:(c + 1) * L, :] = jnp.exp(mlane - mxc)
        coleinv_ref[c * L:(c + 1) * L, :] = jnp.exp(-(pltpu.roll(sc, LANES - H_M, 1) + mxc))
        last = sc[L - 1:L, :]
        mlane = pltpu.roll(last, LANES - H_M, 1) + jnp.maximum(mlane, last)
    m_ref[...] = jnp.broadcast_to(m8, m_ref.shape)
    mlane_ref[...] = jnp.broadcast_to(mlane, mlane_ref.shape)

    tri = (lax.broadcasted_iota(jnp.int32, (L, L), 0)
           >= lax.broadcasted_iota(jnp.int32, (L, L), 1))

    for c in range(n_chunks):
        rows = slice(c * L, (c + 1) * L)
        for h in range(H_M):
            hc_ = slice(h * DH_M, (h + 1) * DH_M)
            q = q_ref[rows, hc_]
            kt = kt_ref[h, :, rows]
            va = va_ref[h, rows, :]
            r_row = r_ref[c, h:h + 1, :]
            w_row = w_ref[c, h:h + 1, :]
            mx_col = colmx_ref[rows, h:h + 1]
            sc_col = colsc_ref[rows, h:h + 1]
            einv_col = coleinv_ref[rows, h:h + 1]
            decay = decays[c][h:h + 1, :]

            pmat = jnp.where(tri, jnp.exp(r_row - mx_col), 0.0)
            s = (_dot(q, kt.astype(BF16)) * pmat).astype(BF16)
            ca = caug_ref[h]
            tot = _dot(s, va) + sc_col * _dot(q, ca.astype(BF16))
            dd = jnp.maximum(jnp.abs(tot[:, DH_M:DH_M + LANES]), einv_col)
            hcell = tot[:, 0:DH_M] / jnp.concatenate([dd, dd], axis=1)
            hm = o_ref[rows, hc_] * _head_norm(hcell)
            hm_ref[rows, hc_] = hm.astype(BF16)

            kw = (kt * w_row).astype(BF16)
            caug_ref[h] = decay * ca + _dot(kw, va)

            if c == 0:
                gb_ref[:, hc_] = _dot_nt(xb_ref[...], wg_ref[D_MODEL + h * DH_M:
                                                             D_MODEL + (h + 1) * DH_M, :])

    @pl.when(t == pl.num_programs(1) - 1)
    def _emit_state():
        c_ref[...] = caug_ref[:, :, 0:DH_M]
        n_ref[...] = caug_ref[:, :, DH_M:DH_M + LANES]

    rb = tile // MIX_OUT_BLOCKS
    for r in range(MIX_OUT_BLOCKS):
        rows = slice(r * rb, (r + 1) * rb)
        yb = _dot(hm_ref[rows, :], wpb_ref[...])
        merged = merged_ref[rows, :] + _sigmoid(gb_ref[rows, :]) * yb
        y = ALPHA * x_ref[rows, :] + _dot(merged.astype(BF16), wmix_ref[...])
        x1_ref[rows, :] = _layer_norm(y, lng_ref[...], lnb_ref[...])


def _mixer_prompt(x, layer, w):
    bsz, seq, _ = x.shape
    tile = min(TILE_T, seq)
    n_tiles = seq // tile
    n_chunks = tile // MIX_CHUNK
    row_block = lambda b, t: (b, t, 0)
    per_seq = lambda b, t: (b, 0, 0)
    per_seq4 = lambda b, t: (b, 0, 0, 0)
    names = ("win", "wgt", "gbias", "wg", "wconv", "mhg", "wpa", "wpb", "wmix",
             "ln1g", "ln1b")
    in_specs = [pl.BlockSpec((None, tile, D_MODEL), row_block)]
    in_specs += [_resident(w[n].shape, layer) for n in names]
    out_shape = (
        jax.ShapeDtypeStruct((bsz, seq, D_MODEL), F32),
        jax.ShapeDtypeStruct((bsz, SUBLANES, D_CONV), F32),
        jax.ShapeDtypeStruct((bsz, H_M, DH_M, DH_M), F32),
        jax.ShapeDtypeStruct((bsz, H_M, DH_M, LANES), F32),
        jax.ShapeDtypeStruct((bsz, SUBLANES, LANES), F32),
    )
    out_specs = (
        pl.BlockSpec((None, tile, D_MODEL), row_block),
        pl.BlockSpec((None, SUBLANES, D_CONV), per_seq),
        pl.BlockSpec((None, H_M, DH_M, DH_M), per_seq4),
        pl.BlockSpec((None, H_M, DH_M, LANES), per_seq4),
        pl.BlockSpec((None, SUBLANES, LANES), per_seq),
    )
    col = pltpu.VMEM((tile, LANES), F32)
    scratch = [
        pltpu.VMEM((tile, D_MODEL), BF16),
        pltpu.VMEM((SUBLANES, D_CONV), F32),
        pltpu.VMEM((tile, D_MODEL), F32),
        pltpu.VMEM((tile, D_M), BF16),
        pltpu.VMEM((H_M, DH_M, tile), F32),
        pltpu.VMEM((H_M, tile, DH_M + LANES), BF16),
        pltpu.VMEM((tile, D_M), F32),
        pltpu.VMEM((tile, D_M), BF16),
        pltpu.VMEM((H_M, DH_M, DH_M + LANES), F32),
        pltpu.VMEM((SUBLANES, LANES), F32),
        pltpu.VMEM((n_chunks, 2 * H_M, MIX_CHUNK), F32),
        pltpu.VMEM((n_chunks, 2 * H_M, MIX_CHUNK), F32),
        col, col, col, col,
        pltpu.VMEM((tile, D_MODEL), F32),
    ]
    x1, sconv, c, n, m = pl.pallas_call(
        _mixer_prompt_kernel,
        grid=(bsz, n_tiles),
        in_specs=in_specs,
        out_specs=out_specs,
        out_shape=out_shape,
        scratch_shapes=scratch,
        compiler_params=pltpu.CompilerParams(
            dimension_semantics=("arbitrary", "arbitrary"),
            vmem_limit_bytes=VMEM_LIMIT_BYTES),
        name="mixer_prompt",
    )(x, *[w[n] for n in names])
    return x1, sconv[:, HALO_ROW:, :], c, n[:, :, :, 0], m[:, :H_M, 0]


def _ffn_prompt_kernel(x_ref, wup_ref, wconv_ref, wdown_ref, lng_ref, lnb_ref,
                       x2_ref, halo_ref,
                       xb_ref, acc_ref):
    t = pl.program_id(1)
    tile = x_ref.shape[0]

    @pl.when(t == 0)
    def _init():
        halo_ref[...] = jnp.zeros(halo_ref.shape, F32)

    xb_ref[...] = x_ref[...].astype(BF16)

    rb = min(FFN_ROW_BLOCK, tile)
    n_row_blocks = tile // rb
    chunk_cols = lambda j: (slice(j * FF_CHUNK, (j + 1) * FF_CHUNK),
                            slice(D_FF + j * FF_CHUNK, D_FF + (j + 1) * FF_CHUNK))

    def up_project(j, r):
        xr = xb_ref[r * rb:(r + 1) * rb, :]
        return tuple(_dot(xr, wup_ref[:, cols]) for cols in chunk_cols(j))

    pending = {(0, r): up_project(0, r) for r in range(n_row_blocks)}
    for j in range(N_FF_CHUNKS):
        prev = [halo_ref[:, cols] for cols in chunk_cols(j)]
        for r in range(n_row_blocks):
            rows = slice(r * rb, (r + 1) * rb)
            vals = pending.pop((j, r))
            g, u = (_causal_conv3(v, p8, wconv_ref[:, cols])
                    for v, p8, cols in zip(vals, prev, chunk_cols(j)))
            prev = [v[rb - SUBLANES:rb, :] for v in vals]
            hdn = (g * _sigmoid(g) * u).astype(BF16)
            if j + 1 < N_FF_CHUNKS:
                pending[(j + 1, r)] = up_project(j + 1, r)
            part = _dot(hdn, wdown_ref[j * FF_CHUNK:(j + 1) * FF_CHUNK, :])
            if j == 0:
                acc_ref[rows, :] = part
            else:
                acc_ref[rows, :] += part
        for cols, tail in zip(chunk_cols(j), prev):
            halo_ref[:, cols] = tail

    y = ALPHA * x_ref[...] + acc_ref[...]
    x2_ref[...] = _layer_norm(y, lng_ref[...], lnb_ref[...])


def _ffn_prompt(x, layer, w):
    bsz, seq, _ = x.shape
    tile = min(FFN_TILE_T, seq)
    n_tiles = seq // tile
    row_block = lambda b, t: (b, t, 0)
    in_specs = [
        pl.BlockSpec((None, tile, D_MODEL), row_block),
        _resident(w["wup"].shape, layer),
        _resident(w["wfconv"].shape, layer),
        _resident(w["wdown"].shape, layer),
        _resident(w["ln2g"].shape, layer),
        _resident(w["ln2b"].shape, layer),
    ]
    x2, halo = pl.pallas_call(
        _ffn_prompt_kernel,
        grid=(bsz, n_tiles),
        in_specs=in_specs,
        out_specs=(pl.BlockSpec((None, tile, D_MODEL), row_block),
                   pl.BlockSpec((None, SUBLANES, 2 * D_FF), lambda b, t: (b, 0, 0))),
        out_shape=(jax.ShapeDtypeStruct((bsz, seq, D_MODEL), F32),
                   jax.ShapeDtypeStruct((bsz, SUBLANES, 2 * D_FF), F32)),
        scratch_shapes=[
            pltpu.VMEM((tile, D_MODEL), BF16),
            pltpu.VMEM((tile, D_MODEL), F32),
        ],
        compiler_params=pltpu.CompilerParams(
            dimension_semantics=("arbitrary", "arbitrary"),
            vmem_limit_bytes=VMEM_LIMIT_BYTES),
        name="ffn_prompt",
    )(x, w["wup"], w["wfconv"], w["wdown"], w["ln2g"], w["ln2b"])
    return x2, halo[:, HALO_ROW:, :]


def _proj_sample_kernel(x_ref, older_ref, newer_ref, win_ref, wgc_ref, gbias_ref, wg_ref,
                        wconv_ref, wpa_ref,
                        q_ref, k_ref, v_ref, so_ref, gates_ref, part_ref, sgb_ref, p_ref):
    xb = x_ref[...].astype(BF16)
    za = _dot_nt(xb, win_ref[0:O_QKVO, :])
    bg = za[:, 0:D_CONV]
    p = za[:, D_CONV:2 * D_CONV] * za[:, 2 * D_CONV:3 * D_CONV]
    wc = wconv_ref[...]
    conv = older_ref[...] * wc[0:1, :] + newer_ref[...] * wc[1:2, :] + p * wc[2:3, :]
    p_ref[...] = p
    ya = _dot((bg * conv).astype(BF16), wpa_ref[...])
    zg = _dot_nt(xb, wg_ref[...])
    part_ref[...] = _sigmoid(zg[:, 0:D_MODEL]) * ya
    sgb_ref[...] = _sigmoid(zg[:, D_MODEL:2 * D_MODEL])
    gz = _dot_nt(xb, wgc_ref[...]) + gbias_ref[...]
    lane = lax.broadcasted_iota(jnp.int32, gz.shape, 1)
    gates_ref[...] = jnp.where(lane < H_M, gz, _log_sigmoid(gz))
    w_rows = lambda i: win_ref[O_QKVO + i * D_M:O_QKVO + (i + 1) * D_M, :]
    q_ref[...] = _dot_nt(xb, w_rows(0))
    k_ref[...] = _dot_nt(xb, w_rows(1)) * K_SCALE
    v_ref[...] = _dot_nt(xb, w_rows(2))
    so_ref[...] = _sigmoid(_dot_nt(xb, w_rows(3)))


def _proj_sample(x, older, newer, layer, w):
    ns = x.shape[0]
    whole = lambda shape: pl.BlockSpec(shape, lambda i: (0,) * len(shape))
    act = jax.ShapeDtypeStruct((ns, D_MODEL), F32)
    in_specs = [
        whole((ns, D_MODEL)),
        pl.BlockSpec((None, ns, D_CONV), lambda i: (layer, 0, 0)),
        pl.BlockSpec((None, ns, D_CONV), lambda i: (layer, 0, 0)),
        _resident(w["win"].shape, layer),
        _resident(w["wgc"].shape, layer),
        _resident(w["gbias_row"].shape, layer),
        _resident(w["wg"].shape, layer),
        _resident(w["wconv"].shape, layer),
        _resident(w["wpa"].shape, layer),
    ]
    out_shape = (act, act, act, act, jax.ShapeDtypeStruct((ns, LANES), F32), act, act,
                 jax.ShapeDtypeStruct((ns, D_CONV), F32))
    out_specs = tuple(whole(s.shape) for s in out_shape)
    return pl.pallas_call(
        _proj_sample_kernel,
        grid=(1,),
        in_specs=in_specs,
        out_specs=out_specs,
        out_shape=out_shape,
        compiler_params=pltpu.CompilerParams(
            dimension_semantics=("arbitrary",), vmem_limit_bytes=VMEM_LIMIT_BYTES),
        name="proj_sample",
    )(x, older, newer, w["win"], w["wgc"], w["gbias_row"], w["wg"], w["wconv"], w["wpa"])


def _state_sample_kernel(q_ref, k_ref, v_ref, so_ref, gates_ref, m0_ref, n0_ref, c0_ref, mhg_ref,
                         *rest):
    hm_ref, c_ref, n_ref, m_ref, qc_ref = rest[-5:]
    nb = q_ref.shape[0]
    gates = gates_ref[...]
    zpad = jnp.zeros((LANES - nb, DH_M), F32)
    for h in range(H_M):
        cols = slice(h * DH_M, (h + 1) * DH_M)
        q = q_ref[:, cols]
        k = k_ref[:, cols]
        v = v_ref[:, cols]
        li = gates[:, h:h + 1]
        lf = gates[:, H_M + h:H_M + h + 1]
        m0 = m0_ref[:, h:h + 1]
        n0 = n0_ref[:, h, :]
        m_new = jnp.maximum(lf + m0, li)
        decay = jnp.exp(lf + m0 - m_new)
        wgt = jnp.exp(li - m_new)
        kw = k * wgt
        qb = q.astype(BF16)
        kw_t = jnp.concatenate([kw, zpad], axis=0).T
        v_pad = jnp.concatenate([v, zpad], axis=0).astype(BF16)
        seq_lane = lax.broadcasted_iota(jnp.int32, kw_t.shape, 1)
        for i in range(nb):
            c0 = c0_ref[i, h]
            qc_ref[i:i + 1, :] = _dot(qb, c0.astype(BF16))[i:i + 1, :]
            kw_i = jnp.where(seq_lane == i, kw_t, 0.0).astype(BF16)
            c_ref[i, h] = decay[i:i + 1, :] * c0 + _dot(kw_i, v_pad)
        s = jnp.sum(q * k, axis=-1, keepdims=True) * wgt
        numer = s * v + decay * qc_ref[...]
        den = s + decay * jnp.sum(q * n0, axis=-1, keepdims=True)
        hc = numer / jnp.maximum(jnp.abs(den), jnp.exp(-m_new))
        hm_ref[:, cols] = so_ref[:, cols] * (_head_norm(hc) * mhg_ref[:, cols])
        n_ref[:, h, :] = decay * n0 + kw
        m_ref[:, h:h + 1] = m_new


def _state_sample(q, k, v, so, gates, m0, n0, c0, layer, w, c_all):
    depth = c0.shape[0]
    ns = q.shape[0]
    nb = min(SEQ_BLOCK, ns)
    rows = lambda i: (i, 0)
    act_spec = pl.BlockSpec((nb, D_MODEL), rows)
    layer_slab = lambda i: (layer, i, 0, 0, 0)
    in_specs = [
        act_spec, act_spec, act_spec, act_spec,
        pl.BlockSpec((nb, LANES), rows),
        pl.BlockSpec((None, nb, H_M), lambda i: (layer, i, 0)),
        pl.BlockSpec((None, nb, H_M, DH_M), lambda i: (layer, i, 0, 0)),
        pl.BlockSpec((None, nb, H_M, DH_M, DH_M), layer_slab),
        pl.BlockSpec((None, 1, D_M), lambda i: (layer, 0, 0)),
    ]
    args = [q, k, v, so, gates, m0, n0, c0, w["mhg"]]
    aliases = {}
    if c_all is not None:
        in_specs.append(pl.BlockSpec(memory_space=pl.ANY))
        args.append(c_all)
        aliases = {len(args) - 1: 1}
    out_shape = (
        jax.ShapeDtypeStruct((ns, D_M), F32),
        jax.ShapeDtypeStruct((depth, ns, H_M, DH_M, DH_M), F32),
        jax.ShapeDtypeStruct((ns, H_M, DH_M), F32),
        jax.ShapeDtypeStruct((ns, H_M), F32),
    )
    out_specs = (
        act_spec,
        pl.BlockSpec((None, nb, H_M, DH_M, DH_M), layer_slab),
        pl.BlockSpec((nb, H_M, DH_M), lambda i: (i, 0, 0)),
        pl.BlockSpec((nb, H_M), rows),
    )
    return pl.pallas_call(
        _state_sample_kernel,
        grid=(ns // nb,),
        in_specs=in_specs,
        out_specs=out_specs,
        out_shape=out_shape,
        input_output_aliases=aliases,
        scratch_shapes=[pltpu.VMEM((nb, DH_M), F32)],
        compiler_params=pltpu.CompilerParams(
            dimension_semantics=("arbitrary",), vmem_limit_bytes=VMEM_LIMIT_BYTES),
        name="state_sample",
    )(*args)


def _out_sample_kernel(x_ref, hm_ref, part_ref, sgb_ref, older_g, older_u, newer_g, newer_u,
                       wpb_ref, wmix_ref, ln1g_ref, ln1b_ref, wup_g, wup_u, wconv_g, wconv_u,
                       wdown_ref, ln2g_ref, ln2b_ref,
                       x2_ref, upg_ref, upu_ref,
                       x1_ref, xb_ref, acc_ref):
    k = pl.program_id(0)

    @pl.when(k == 0)
    def _mix():
        yb = _dot(hm_ref[...].astype(BF16), wpb_ref[...])
        merged = part_ref[...] + sgb_ref[...] * yb
        y = ALPHA * x_ref[...] + _dot(merged.astype(BF16), wmix_ref[...])
        x1 = _layer_norm(y, ln1g_ref[...], ln1b_ref[...])
        x1_ref[...] = x1
        xb_ref[...] = x1.astype(BF16)
        acc_ref[...] = jnp.zeros(acc_ref.shape, F32)

    def conv(older, newer, wup, wconv, up_out):
        val = _dot(xb_ref[...], wup[...])
        up_out[...] = val
        wc = wconv[...]
        return older[...] * wc[0:1, :] + newer[...] * wc[1:2, :] + val * wc[2:3, :]

    g = conv(older_g, newer_g, wup_g, wconv_g, upg_ref)
    u = conv(older_u, newer_u, wup_u, wconv_u, upu_ref)
    acc_ref[...] += _dot((g * _sigmoid(g) * u).astype(BF16), wdown_ref[...])

    @pl.when(k == N_FF_CHUNKS - 1)
    def _finish():
        x2_ref[...] = _layer_norm(ALPHA * x1_ref[...] + acc_ref[...], ln2g_ref[...], ln2b_ref[...])


def _out_sample(x, hm, part, sgb, older, newer, layer, w):
    ns = x.shape[0]
    act = pl.BlockSpec((ns, D_MODEL), lambda k: (0, 0))
    gate_cols = lambda k: (layer, 0, k)
    value_cols = lambda k: (layer, 0, N_FF_CHUNKS + k)
    both = lambda shape: [pl.BlockSpec(shape, gate_cols), pl.BlockSpec(shape, value_cols)]
    in_specs = (
        [act, act, act, act]
        + both((None, ns, FF_CHUNK)) + both((None, ns, FF_CHUNK))
        + [_resident(w[n].shape, layer) for n in ("wpb", "wmix", "ln1g", "ln1b")]
        + both((None, D_MODEL, FF_CHUNK)) + both((None, CONV_W, FF_CHUNK))
        + [pl.BlockSpec((None, FF_CHUNK, D_MODEL), lambda k: (layer, k, 0)),
           _resident(w["ln2g"].shape, layer), _resident(w["ln2b"].shape, layer)])
    up_half = jax.ShapeDtypeStruct((ns, D_FF), F32)
    up_spec = pl.BlockSpec((ns, FF_CHUNK), lambda k: (0, k))
    x2, up_g, up_u = pl.pallas_call(
        _out_sample_kernel,
        grid=(N_FF_CHUNKS,),
        in_specs=in_specs,
        out_specs=(act, up_spec, up_spec),
        out_shape=(jax.ShapeDtypeStruct((ns, D_MODEL), F32), up_half, up_half),
        scratch_shapes=[
            pltpu.VMEM((ns, D_MODEL), F32),
            pltpu.VMEM((ns, D_MODEL), BF16),
            pltpu.VMEM((ns, D_MODEL), F32),
        ],
        compiler_params=pltpu.CompilerParams(
            dimension_semantics=("arbitrary",), vmem_limit_bytes=VMEM_LIMIT_BYTES),
        name="out_sample",
    )(x, hm, part, sgb, older, older, newer, newer, w["wpb"], w["wmix"], w["ln1g"], w["ln1b"],
      w["wup"], w["wup"], w["wfconv"], w["wfconv"], w["wdown"], w["ln2g"], w["ln2b"])
    return x2, jnp.concatenate([up_g, up_u], axis=-1)


def _pack_weights(w_in, b_igate, b_fgate, w_conv_mix, mhln_g, w_proj_a, w_proj_b, w_mix_out,
                  ln1_g, ln1_b, w_ffn_up, w_ffn_conv, w_ffn_down, ln2_g, ln2_b):
    depth = w_in.shape[0]
    o_a = 3 * D_CONV
    o_gate = o_a + 4 * D_M
    o_g = o_gate + 2 * H_M
    w_t = jnp.swapaxes(w_in, 1, 2).astype(BF16)
    wgate = w_t[:, o_gate:o_g, :]
    gbias = jnp.concatenate([b_igate, b_fgate], axis=-1)
    gbias_row = jnp.pad(gbias, ((0, 0), (0, LANES - 2 * H_M)))[:, None, :]

    row = lambda a: a[:, None, :]
    return {
        "win": w_t,
        "wgt": jnp.concatenate([w_t[:, o_a + D_M:o_a + D_M + DH_M, :],
                                jnp.pad(wgate, ((0, 0), (0, 2 * H_M), (0, 0)))], axis=1),
        "gbias": gbias[:, :, None],
        "wgc": jnp.pad(wgate, ((0, 0), (0, LANES - 2 * H_M), (0, 0))),
        "gbias_row": gbias_row,
        "wg": w_t[:, o_g:o_g + 2 * D_MODEL, :],
        "wconv": w_conv_mix,
        "mhg": row(mhln_g),
        "wpa": w_proj_a.astype(BF16),
        "wpb": w_proj_b.astype(BF16),
        "wmix": w_mix_out.astype(BF16),
        "ln1g": row(ln1_g), "ln1b": row(ln1_b),
        "wup": w_ffn_up.astype(BF16),
        "wfconv": w_ffn_conv,
        "wdown": w_ffn_down.astype(BF16),
        "ln2g": row(ln2_g), "ln2b": row(ln2_b),
    }


def kernel(x_prompt, x_sample, cache_sconv, state_mlstm_C, state_mlstm_n, state_mlstm_m, cache_ffn_conv, w_in, b_igate, b_fgate, w_conv_mix, mhln_g, w_proj_a, w_proj_b, w_mix_out, ln1_g, ln1_b, w_ffn_up, w_ffn_conv, w_ffn_down, ln2_g, ln2_b):
    depth = w_in.shape[0]
    ns = x_sample.shape[0]
    seq = x_prompt.shape[1]
    assert x_sample.shape[1] == 1
    assert seq % min(TILE_T, seq) == 0 and min(TILE_T, seq) % MIX_CHUNK == 0
    assert seq % min(FFN_TILE_T, seq) == 0 and min(FFN_TILE_T, seq) % FFN_ROW_BLOCK == 0
    assert ns % min(SEQ_BLOCK, ns) == 0
    w = _pack_weights(w_in, b_igate, b_fgate, w_conv_mix, mhln_g, w_proj_a, w_proj_b, w_mix_out,
                      ln1_g, ln1_b, w_ffn_up, w_ffn_conv, w_ffn_down, ln2_g, ln2_b)

    sconv_old, sconv_new = cache_sconv[:, :, 0, :], cache_sconv[:, :, 1, :]
    ffn_old, ffn_new = cache_ffn_conv[:, :, 0, :], cache_ffn_conv[:, :, 1, :]

    xp = x_prompt
    xs = x_sample.reshape(ns, D_MODEL)
    outs = [[] for _ in range(9)]
    c_sample = None
    for l in range(depth):
        xp, sp, cp, np_, mp = _mixer_prompt(xp, l, w)
        xp, fp = _ffn_prompt(xp, l, w)

        q, k, v, so, gates, part, sgb, ss = _proj_sample(xs, sconv_old, sconv_new, l, w)
        hm, c_sample, ns_, ms = _state_sample(q, k, v, so, gates, state_mlstm_m, state_mlstm_n,
                                              state_mlstm_C, l, w, c_sample)
        xs, fs = _out_sample(xs, hm, part, sgb, ffn_old, ffn_new, l, w)

        for lst, val in zip(outs, (sp, ss, cp, np_, ns_, mp, ms, fp, fs)):
            lst.append(val)

    sp, ss, cp, np_, ns_, mp, ms, fp, fs = (jnp.stack(o) for o in outs)
    age = lax.broadcasted_iota(jnp.int32, (1, 1, CONV_W - 1, 1), 2)
    ss = jnp.where(age == 0, sconv_new[:, :, None, :], ss[:, :, None, :])
    fs = jnp.where(age == 0, ffn_new[:, :, None, :], fs[:, :, None, :])
    return (xp, xs.reshape(ns, 1, D_MODEL), sp, ss, cp, c_sample, np_, ns_, mp, ms, fp, fs)
```
